```python
import math
import jax
import jax.numpy as jnp
from jax import lax
import numpy as np

D_MODEL = 1024
BATCH = 2
SEQ = 16384
DEPTH = 2

CTX_LEN = 256
GRID_W = 64
N_EVEN = (DEPTH + 1) // 2
N_ODD = DEPTH // 2
EPS = 1e-6

HY_WIDTH = D_MODEL // 2
HY_ORDER = 2
HY_SHORT_K = 3
HY_EMB_BANDS = 16
HY_EMB_DIM = 1 + 2 * HY_EMB_BANDS
HY_FILTER_HIDDEN = 64
HY_DECAY_TARGET = 1e-2
HY_FAST_DECAY_PCT = 0.3
HY_SLOW_DECAY_PCT = 1.5

DA_HEAD_DIM = 64
DA_HEADS = (D_MODEL // 2) // (2 * DA_HEAD_DIM)
DA_WIDTH = DA_HEADS * 2 * DA_HEAD_DIM
ROPE_THETA = 10000.0
ATTN_BLOCK = 128

EVEN_IN = 4 * HY_WIDTH + 4 * DA_WIDTH
EVEN_MIX = HY_WIDTH + DA_WIDTH
EVEN_SPLITS = (3 * HY_WIDTH, 4 * HY_WIDTH, 4 * HY_WIDTH + DA_WIDTH,
               4 * HY_WIDTH + 2 * DA_WIDTH, 4 * HY_WIDTH + 3 * DA_WIDTH)

SSD_INNER = 2 * D_MODEL
SSD_HEAD_DIM = 64
SSD_HEADS = SSD_INNER // SSD_HEAD_DIM
SSD_GROUPS = 8
SSD_STATE = 128
SSD_CONV_K = 3
SSD_CHUNK = 128
SSD_CONV_CH = SSD_INNER + 2 * SSD_GROUPS * SSD_STATE
ODD_IN = SSD_INNER + SSD_CONV_CH + 2 * SSD_HEADS

kernel_name = "hybrid_hyena_diffattn_ssd_prefix_dit"


def rmsnorm(x, w):
    xf = x.astype(jnp.float32)
    y = xf * lax.rsqrt(jnp.mean(xf * xf, axis=-1, keepdims=True) + EPS)
    return (y * w.astype(jnp.float32)).astype(x.dtype)


def dwconv_centred(x, w, b):
    K = w.shape[0]
    r = K // 2
    L = x.shape[1]
    xp = jnp.pad(x, ((0, 0), (r, r), (0, 0)))
    return sum(xp[:, j:j + L] * w[j] for j in range(K)) + b


def axial_rope_tables(rows, head_dim):
    r, col = jnp.meshgrid(jnp.arange(rows), jnp.arange(GRID_W), indexing="ij")
    r = r.reshape(-1).astype(jnp.float32)
    col = col.reshape(-1).astype(jnp.float32)
    axis_dim = head_dim // 2
    inv = ROPE_THETA ** (-jnp.arange(0, axis_dim, 2, dtype=jnp.float32) / axis_dim)
    ar = r[:, None] * inv
    ac = col[:, None] * inv
    return (jnp.cos(ar), jnp.sin(ar), jnp.cos(ac), jnp.sin(ac))


def _rotate(x, cos, sin):
    x1, x2 = jnp.split(x, 2, axis=-1)
    return jnp.concatenate([x1 * cos - x2 * sin, x1 * sin + x2 * cos], axis=-1)


def apply_axial_rope(x, tables):
    cr, sr, cc, sc = [t[:, None, None, :].astype(x.dtype) for t in tables]
    half = x.shape[-1] // 2
    return jnp.concatenate([_rotate(x[..., :half], cr, sr), _rotate(x[..., half:], cc, sc)], axis=-1)


def hyena_freq_response(L, f_w1, f_b1, f_w2, f_b2, f_w3, f_b3, f_freq):
    f32 = jnp.float32
    t = jnp.linspace(0.0, 1.0, L, dtype=f32)[:, None]
    w = 2.0 * math.pi * jnp.arange(L, dtype=f32)[:, None] / L
    bands = jnp.linspace(1e-4, HY_EMB_BANDS - 1, HY_EMB_BANDS, dtype=f32)
    z = jnp.concatenate([t, jnp.cos(bands * w), -jnp.sin(bands * w)], axis=-1)
    freq = f_freq.astype(f32)
    h = jnp.sin(freq[0] * (z @ f_w1.astype(f32) + f_b1.astype(f32)))
    h = jnp.sin(freq[1] * (h @ f_w2.astype(f32) + f_b2.astype(f32)))
    h = h @ f_w3.astype(f32) + f_b3.astype(f32)
    min_decay = math.log(HY_DECAY_TARGET) / HY_SLOW_DECAY_PCT
    max_decay = math.log(HY_DECAY_TARGET) / HY_FAST_DECAY_PCT
    deltas = jnp.abs(jnp.linspace(min_decay, max_decay, HY_WIDTH, dtype=f32))
    window = jnp.exp(-t * deltas)
    h = h.reshape(L, 2, HY_ORDER, HY_WIDTH) * window[:, None, None, :]
    k = jnp.concatenate([h[:, 0], jnp.zeros_like(h[:1, 0]), h[:0:-1, 1]], axis=0)
    k = k / jnp.sum(jnp.abs(k), axis=0, keepdims=True)
    return jnp.fft.rfft(k, axis=0)


def fft_long_conv(v, k_f, bias):
    L = v.shape[1]
    vf = v.astype(jnp.float32)
    y = jnp.fft.irfft(jnp.fft.rfft(vf, n=2 * L, axis=1) * k_f, n=2 * L, axis=1)[:, :L]
    return (y + vf * bias.astype(jnp.float32)).astype(v.dtype)


def hyena_mix(u, short_w, short_b, filt, hy_bias):
    L = u.shape[1]
    u = dwconv_centred(u, short_w, short_b)
    v, x1, x2 = jnp.split(u, 3, axis=-1)
    k_f = hyena_freq_response(L, *filt)
    z = fft_long_conv(v, k_f[:, 0], hy_bias[0]) * x1
    return fft_long_conv(z, k_f[:, 1], hy_bias[1]) * x2


def diff_attend(q, k, v, lam):
    s = jnp.einsum("bqhmd,bkhmd->bhmqk", q, k).astype(jnp.float32) * (DA_HEAD_DIM ** -0.5)
    p = jax.nn.softmax(s, axis=-1)
    a = p[:, :, 0] - lam * p[:, :, 1]
    return jnp.einsum("bhqk,bkhe->bqhe", a.astype(v.dtype), v)


def ssd_chunked(x, dt, A, Bm, Cm, D, init_state):
    f32 = jnp.float32
    b, L, h, p = x.shape
    nc = L // SSD_CHUNK
    hg = h // SSD_GROUPS
    xf = x.astype(f32)
    dtf = dt.astype(f32)
    a = dtf * A.astype(f32)
    xdt = xf * dtf[..., None]

    def chunks(t):
        return t.reshape((b, nc, SSD_CHUNK) + t.shape[2:]).swapaxes(0, 1)

    xs = chunks(xdt.reshape(b, L, SSD_GROUPS, hg, p))
    a_s = chunks(a.reshape(b, L, SSD_GROUPS, hg))
    Bs = chunks(Bm.astype(f32))
    Cs = chunks(Cm.astype(f32))
    mask = jnp.tril(jnp.ones((SSD_CHUNK, SSD_CHUNK), dtype=bool))[None, :, :, None, None]

    def step(state, inp):
        xc, ac, Bc, Cc = inp
        acum = jnp.cumsum(ac, axis=1)
        seg = acum[:, :, None] - acum[:, None, :]
        decay = jnp.exp(jnp.where(mask, seg, -jnp.inf))
        cb = jnp.einsum("blgn,bsgn->blsg", Cc, Bc)
        y = jnp.einsum("blsgk,bsgkp->blgkp", cb[..., None] * decay, xc)
        y = y + jnp.einsum("blgn,bgkpn->blgkp", Cc, state) * jnp.exp(acum)[..., None]
        last = acum[:, -1]
        to_end = jnp.exp(last[:, None] - acum)
        state = state * jnp.exp(last)[..., None, None] + jnp.einsum(
            "bsgn,bsgk,bsgkp->bgkpn", Bc, to_end, xc)
        return state, y

    state0 = init_state.astype(f32).reshape(b, SSD_GROUPS, hg, p, SSD_STATE)
    final, ys = lax.scan(step, state0, (xs, a_s, Bs, Cs))
    y = ys.swapaxes(0, 1).reshape(b, L, h, p) + xf * D.astype(f32)[:, None]
    return y, final.reshape(b, h, p, SSD_STATE)


def bidir_ssd(x, dt, A, Bm, Cm, D, s_fwd, s_bwd):
    y_f, s_f = ssd_chunked(x, dt[:, :, 0], A[0], Bm, Cm, D[0], s_fwd)
    rev = lambda t: jnp.flip(t, axis=1)
    y_b, s_b = ssd_chunked(rev(x), rev(dt[:, :, 1]), A[1], rev(Bm), rev(Cm), D[1], s_bwd)
    return y_f + rev(y_b), s_f, s_b


def even_mixer(xn_c, xn_l, rope, layer_idx, need_ctx, in_w, out_w, short_w, short_b,
               f_w1, f_b1, f_w2, f_b2, f_w3, f_b3, f_freq, hy_bias, lam_p, subln_w):
    filt = (f_w1, f_b1, f_w2, f_b2, f_w3, f_b3, f_freq)
    lam_init = 0.8 - 0.6 * math.exp(-0.3 * layer_idx)
    lp = lam_p.astype(jnp.float32)
    lam = jnp.exp(jnp.sum(lp[0] * lp[1])) - jnp.exp(jnp.sum(lp[2] * lp[3])) + lam_init

    def project(xn):
        b, L = xn.shape[:2]
        hy, hg, q, k, v, ag = jnp.split(xn @ in_w, list(EVEN_SPLITS), axis=-1)
        q = q.reshape(b, L, DA_HEADS, 2, DA_HEAD_DIM)
        k = k.reshape(b, L, DA_HEADS, 2, DA_HEAD_DIM)
        v = v.reshape(b, L, DA_HEADS, 2 * DA_HEAD_DIM)
        return hy, hg, q, k, v, ag

    def merge(hy_out, hg, att, ag):
        b, L = att.shape[:2]
        att = (rmsnorm(att, subln_w) * (1.0 - lam_init)).reshape(b, L, DA_WIDTH)
        mix = jnp.concatenate([hy_out * jax.nn.silu(hg), att * jax.nn.silu(ag)], axis=-1)
        return mix @ out_w

    hy_c, hg_c, q_c, k_c, v_c, ag_c = project(xn_c)
    hy_l, hg_l, q_l, k_l, v_l, ag_l = project(xn_l)

    b, L = xn_l.shape[:2]
    q_l = apply_axial_rope(q_l, rope)
    k_all = jnp.concatenate([k_c, apply_axial_rope(k_l, rope)], axis=1)
    v_all = jnp.concatenate([v_c, v_l], axis=1)
    nb = L // ATTN_BLOCK
    qb = q_l.reshape(b, nb, ATTN_BLOCK, DA_HEADS, 2, DA_HEAD_DIM).swapaxes(0, 1)
    att_l = lax.map(lambda qq: diff_attend(qq, k_all, v_all, lam), qb)
    att_l = att_l.swapaxes(0, 1).reshape(b, L, DA_HEADS, 2 * DA_HEAD_DIM)
    out_l = merge(hyena_mix(hy_l, short_w, short_b, filt, hy_bias), hg_l, att_l, ag_l)

    out_c = None
    if need_ctx:
        att_c = diff_attend(q_c, k_c, v_c, lam)
        out_c = merge(hyena_mix(hy_c, short_w, short_b, filt, hy_bias), hg_c, att_c, ag_c)
    return out_c, out_l


def odd_mixer(xn_c, xn_l, need_ctx, in_w, conv_w, conv_b, dt_bias, A_log, D, gnorm_w, out_w):
    f32 = jnp.float32
    A = -jnp.exp(A_log.astype(f32))

    def prep(xn):
        b, L = xn.shape[:2]
        z, xbc, dt = jnp.split(xn @ in_w, [SSD_INNER, SSD_INNER + SSD_CONV_CH], axis=-1)
        xbc = jax.nn.silu(dwconv_centred(xbc, conv_w, conv_b))
        xs, Bm, Cm = jnp.split(xbc, [SSD_INNER, SSD_INNER + SSD_GROUPS * SSD_STATE], axis=-1)
        xs = xs.reshape(b, L, SSD_HEADS, SSD_HEAD_DIM)
        Bm = Bm.reshape(b, L, SSD_GROUPS, SSD_STATE)
        Cm = Cm.reshape(b, L, SSD_GROUPS, SSD_STATE)
        dt = jax.nn.softplus(dt.astype(f32).reshape(b, L, 2, SSD_HEADS) + dt_bias.astype(f32))
        return z, xs, Bm, Cm, dt

    def finish(y, z):
        b, L = z.shape[:2]
        y = y.reshape(b, L, SSD_INNER).astype(z.dtype) * jax.nn.silu(z)
        y = rmsnorm(y.reshape(b, L, SSD_GROUPS, SSD_INNER // SSD_GROUPS),
                    gnorm_w.reshape(SSD_GROUPS, SSD_INNER // SSD_GROUPS)).reshape(b, L, SSD_INNER)
        return y @ out_w

    z_c, x_c, B_c, C_c, dt_c = prep(xn_c)
    s0 = jnp.zeros((xn_c.shape[0], SSD_HEADS, SSD_HEAD_DIM, SSD_STATE), f32)
    y_c, s_f, s_b = bidir_ssd(x_c, dt_c, A, B_c, C_c, D, s0, s0)
    z_l, x_l, B_l, C_l, dt_l = prep(xn_l)
    y_l, _, _ = bidir_ssd(x_l, dt_l, A, B_l, C_l, D, s_f, s_b)
    out_l = finish(y_l, z_l)
    out_c = finish(y_c, z_c) if need_ctx else None
    return out_c, out_l


def setup_inputs(seed: int = 0) -> dict:
    key = jax.random.key(seed)
    ks = jax.random.split(key, 32)
    f32 = jnp.float32
    D = D_MODEL

    def nrm(i, shape, scale):
        return jax.random.normal(ks[i], shape, f32) * scale

    x = nrm(0, (BATCH, SEQ, D), 1.0)
    c = nrm(1, (BATCH, D), 1.0)
    ctx = nrm(2, (BATCH, CTX_LEN, D), 1.0)
    c_ctx = nrm(3, (D,), 1.0)
    ada_w = nrm(4, (DEPTH, D, 3 * D), 0.5 * D ** -0.5)
    ada_b = nrm(5, (DEPTH, 3 * D), 0.02)
    norm_w = 1.0 + nrm(6, (DEPTH, D), 0.02)
    ev_in_w = nrm(7, (N_EVEN, D, EVEN_IN), D ** -0.5)
    ev_out_w = nrm(8, (N_EVEN, EVEN_MIX, D), EVEN_MIX ** -0.5)
    hy_short_w = nrm(9, (N_EVEN, HY_SHORT_K, 3 * HY_WIDTH), HY_SHORT_K ** -0.5)
    hy_short_b = nrm(10, (N_EVEN, 3 * HY_WIDTH), 0.02)
    hy_f_w1 = nrm(11, (N_EVEN, HY_EMB_DIM, HY_FILTER_HIDDEN), HY_EMB_DIM ** -0.5)
    hy_f_b1 = nrm(12, (N_EVEN, HY_FILTER_HIDDEN), 0.1)
    hy_f_w2 = nrm(13, (N_EVEN, HY_FILTER_HIDDEN, HY_FILTER_HIDDEN), HY_FILTER_HIDDEN ** -0.5)
    hy_f_b2 = nrm(14, (N_EVEN, HY_FILTER_HIDDEN), 0.1)
    hy_f_w3 = nrm(15, (N_EVEN, HY_FILTER_HIDDEN, 2 * HY_ORDER * HY_WIDTH), HY_FILTER_HIDDEN ** -0.5)
    hy_f_b3 = nrm(16, (N_EVEN, 2 * HY_ORDER * HY_WIDTH), 0.02)
    hy_freq = 1.0 + nrm(17, (N_EVEN, 2, HY_FILTER_HIDDEN), 0.1)
    hy_bias = nrm(18, (N_EVEN, HY_ORDER, HY_WIDTH), 0.5)
    da_lambda = nrm(19, (N_EVEN, 4, DA_HEAD_DIM), 0.1)
    da_subln_w = 1.0 + nrm(20, (N_EVEN, 2 * DA_HEAD_DIM), 0.02)
    od_in_w = nrm(21, (N_ODD, D, ODD_IN), D ** -0.5)
    od_conv_w = nrm(22, (N_ODD, SSD_CONV_K, SSD_CONV_CH), SSD_CONV_K ** -0.5)
    od_conv_b = nrm(23, (N_ODD, SSD_CONV_CH), 0.02)
    dt0 = jnp.exp(jax.random.uniform(ks[24], (N_ODD, 2, SSD_HEADS), f32,
                                     math.log(1e-3), math.log(1e-1)))
    od_dt_bias = dt0 + jnp.log(-jnp.expm1(-dt0))
    od_A_log = jnp.log(jax.random.uniform(ks[25], (N_ODD, 2, SSD_HEADS), f32, 1.0, 16.0))
    od_D = 1.0 + nrm(26, (N_ODD, 2, SSD_HEADS), 0.1)
    od_norm_w = 1.0 + nrm(27, (N_ODD, SSD_INNER), 0.02)
    od_out_w = nrm(28, (N_ODD, SSD_INNER, D), SSD_INNER ** -0.5)
    final_norm_w = 1.0 + nrm(29, (D,), 0.02)
    return {"x": x, "c": c, "ctx": ctx, "c_ctx": c_ctx, "ada_w": ada_w, "ada_b": ada_b,
            "norm_w": norm_w, "ev_in_w": ev_in_w, "ev_out_w": ev_out_w,
            "hy_short_w": hy_short_w, "hy_short_b": hy_short_b,
            "hy_f_w1": hy_f_w1, "hy_f_b1": hy_f_b1, "hy_f_w2": hy_f_w2, "hy_f_b2": hy_f_b2,
            "hy_f_w3": hy_f_w3, "hy_f_b3": hy_f_b3, "hy_freq": hy_freq, "hy_bias": hy_bias,
            "da_lambda": da_lambda, "da_subln_w": da_subln_w,
            "od_in_w": od_in_w, "od_conv_w": od_conv_w, "od_conv_b": od_conv_b,
            "od_dt_bias": od_dt_bias, "od_A_log": od_A_log, "od_D": od_D,
            "od_norm_w": od_norm_w, "od_out_w": od_out_w, "final_norm_w": final_norm_w}


def reference(x, c, ctx, c_ctx, ada_w, ada_b, norm_w, ev_in_w, ev_out_w, hy_short_w, hy_short_b,
              hy_f_w1, hy_f_b1, hy_f_w2, hy_f_b2, hy_f_w3, hy_f_b3, hy_freq, hy_bias,
              da_lambda, da_subln_w, od_in_w, od_conv_w, od_conv_b, od_dt_bias, od_A_log, od_D,
              od_norm_w, od_out_w, final_norm_w):
    n_lat = x.shape[1]
    rows = n_lat // GRID_W
    rope = axial_rope_tables(rows, DA_HEAD_DIM)
    silu_c = jax.nn.silu(c)
    silu_cc = jax.nn.silu(c_ctx)
    h_lat, h_ctx = x, ctx
    for i in range(DEPTH):
        need_ctx = i < DEPTH - 1
        mod_l = silu_c @ ada_w[i] + ada_b[i]
        mod_c = silu_cc @ ada_w[i] + ada_b[i]
        sh_l, sc_l, g_l = jnp.split(mod_l, 3, axis=-1)
        sh_c, sc_c, g_c = jnp.split(mod_c, 3, axis=-1)
        xn_l = rmsnorm(h_lat, norm_w[i]) * (1.0 + sc_l[:, None]) + sh_l[:, None]
        xn_c = rmsnorm(h_ctx, norm_w[i]) * (1.0 + sc_c) + sh_c
        j = i // 2
        if i % 2 == 0:
            out_c, out_l = even_mixer(
                xn_c, xn_l, rope, i, need_ctx, ev_in_w[j], ev_out_w[j], hy_short_w[j], hy_short_b[j],
                hy_f_w1[j], hy_f_b1[j], hy_f_w2[j], hy_f_b2[j], hy_f_w3[j], hy_f_b3[j], hy_freq[j],
                hy_bias[j], da_lambda[j], da_subln_w[j])
        else:
            out_c, out_l = odd_mixer(
                xn_c, xn_l, need_ctx, od_in_w[j], od_conv_w[j], od_conv_b[j], od_dt_bias[j],
                od_A_log[j], od_D[j], od_norm_w[j], od_out_w[j])
        h_lat = h_lat + g_l[:, None] * out_l
        if need_ctx:
            h_ctx = h_ctx + g_c * out_c
    return rmsnorm(h_lat, final_norm_w)
```

```python
import functools
import math

import jax
import jax.numpy as jnp
import numpy as np
from jax import lax
from jax.experimental import pallas as pl
from jax.experimental.pallas import tpu as pltpu

F32 = jnp.float32
BF16 = jnp.bfloat16
HI = lax.Precision.HIGHEST

EPS = 1e-6
GRID_W = 64
ROPE_THETA = 10000.0
DA_HEAD_DIM = 64
HY_EMB_BANDS = 16
HY_DECAY_TARGET = 1e-2
HY_FAST_DECAY_PCT = 0.3
HY_SLOW_DECAY_PCT = 1.5
SSD_HEAD_DIM = 64
SSD_GROUPS = 8
SSD_STATE = 128
SSD_CHUNK = 128

LANES = 128
VMEM_LIMIT = 56 * 1024 * 1024


def _row_tile(t, cands=(1280, 1024, 640, 512, 256, 128)):
    return next(c for c in cands if t % c == 0)


def _cparams(sem):
    return pltpu.CompilerParams(dimension_semantics=sem, vmem_limit_bytes=VMEM_LIMIT)


def _silu(x):
    return x * (1.0 / (1.0 + jnp.exp(-x)))


def _nt_dot(a, b):
    return lax.dot_general(a, b, (((1,), (1,)), ((), ())), preferred_element_type=F32)


def _mod_kernel(c_ref, w_ref, b_ref, o_ref):
    s = _silu(c_ref[...])
    o_ref[...] = jnp.dot(s, w_ref[...], precision=HI, preferred_element_type=F32) + b_ref[...]


def _modulation(cstack, ada_w, ada_b):
    depth, d, n = ada_w.shape
    tn = 1024
    return pl.pallas_call(
        _mod_kernel,
        out_shape=jax.ShapeDtypeStruct((depth, 8, n), F32),
        grid=(depth, n // tn),
        in_specs=[pl.BlockSpec((8, d), lambda i, j: (0, 0)),
                  pl.BlockSpec((None, d, tn), lambda i, j: (i, 0, j)),
                  pl.BlockSpec((None, 1, tn), lambda i, j: (i, 0, j))],
        out_specs=pl.BlockSpec((None, 8, tn), lambda i, j: (i, 0, j)),
        compiler_params=_cparams(("parallel", "parallel")),
        name="adaln_mod",
    )(cstack, ada_w, ada_b.reshape(depth, 1, n))


def _row_select(i, tm, nctx, ctx_val, lat_val):
    rows = i * tm + lax.broadcasted_iota(jnp.int32, (tm, 1), 0)
    return jnp.where(rows < nctx, ctx_val, lat_val)


def _normproj_kernel(h_ref, ml_ref, mc_ref, nw_ref, w_ref, o_ref, xn_ref, *, tm, nctx, d):
    i = pl.program_id(1)

    @pl.when(pl.program_id(2) == 0)
    def _():
        x = h_ref[...]
        y = x * lax.rsqrt(jnp.mean(x * x, axis=-1, keepdims=True) + EPS) * nw_ref[...]
        sh = _row_select(i, tm, nctx, mc_ref[:, 0:d], ml_ref[:, 0:d])
        sc = _row_select(i, tm, nctx, mc_ref[:, d:2 * d], ml_ref[:, d:2 * d])
        xn_ref[...] = (y * (1.0 + sc) + sh).astype(BF16)

    o_ref[...] = jnp.dot(xn_ref[...], w_ref[...], preferred_element_type=F32).astype(o_ref.dtype)


def _normproj(h, modl, modc, norm_w, w, *, nctx, tn=1024, out_dtype=F32):
    b, t, d = h.shape
    n = w.shape[1]
    tn = min(tn, n)
    tm = _row_tile(t)
    assert n % tn == 0
    return pl.pallas_call(
        functools.partial(_normproj_kernel, tm=tm, nctx=nctx, d=d),
        out_shape=jax.ShapeDtypeStruct((b, t, n), out_dtype),
        grid=(b, t // tm, n // tn),
        in_specs=[pl.BlockSpec((None, tm, d), lambda bi, i, j: (bi, i, 0)),
                  pl.BlockSpec((None, 1, 3 * d), lambda bi, i, j: (bi, 0, 0)),
                  pl.BlockSpec((1, 3 * d), lambda bi, i, j: (0, 0)),
                  pl.BlockSpec((1, d), lambda bi, i, j: (0, 0)),
                  pl.BlockSpec((d, tn), lambda bi, i, j: (0, j))],
        out_specs=pl.BlockSpec((None, tm, tn), lambda bi, i, j: (bi, i, j)),
        scratch_shapes=[pltpu.VMEM((tm, d), BF16)],
        compiler_params=_cparams(("parallel", "parallel", "arbitrary")),
        name="normproj",
    )(h, modl, modc, norm_w.reshape(1, d), w)


def _dwconv_kernel(x_ref, p_ref, n_ref, w_ref, b_ref, o_ref, *, tm, nctx, t_total, act):
    i = pl.program_id(1)
    x = x_ref[...]
    lrow = lax.broadcasted_iota(jnp.int32, (tm, 1), 0)
    grow = i * tm + lrow
    up = jnp.where(lrow == 0, p_ref[7:8, :], pltpu.roll(x, 1, 0))
    up = jnp.where((grow == 0) | (grow == nctx), 0.0, up)
    dn = jnp.where(lrow == tm - 1, n_ref[0:1, :], pltpu.roll(x, tm - 1, 0))
    dn = jnp.where((grow == nctx - 1) | (grow == t_total - 1), 0.0, dn)
    y = up * w_ref[0:1, :] + x * w_ref[1:2, :] + dn * w_ref[2:3, :] + b_ref[...]
    if act:
        y = _silu(y)
    o_ref[...] = y


def _dwconv3(x, w, bias, *, col0, ncol, cw, nctx, act):
    b, t, _ = x.shape
    tm = _row_tile(t, (640, 512, 256, 128))
    r8 = tm // 8
    nblk8 = t // 8
    return pl.pallas_call(
        functools.partial(_dwconv_kernel, tm=tm, nctx=nctx, t_total=t, act=act),
        out_shape=jax.ShapeDtypeStruct((b, t, ncol * cw), F32),
        grid=(b, t // tm, ncol),
        in_specs=[pl.BlockSpec((None, tm, cw), lambda bi, i, j: (bi, i, col0 + j)),
                  pl.BlockSpec((None, 8, cw), lambda bi, i, j: (bi, jnp.maximum(i * r8 - 1, 0), col0 + j)),
                  pl.BlockSpec((None, 8, cw), lambda bi, i, j: (bi, jnp.minimum((i + 1) * r8, nblk8 - 1), col0 + j)),
                  pl.BlockSpec((3, cw), lambda bi, i, j: (0, j)),
                  pl.BlockSpec((1, cw), lambda bi, i, j: (0, j))],
        out_specs=pl.BlockSpec((None, tm, cw), lambda bi, i, j: (bi, i, j)),
        compiler_params=_cparams(("parallel", "parallel", "parallel")),
        name="dwconv3",
    )(x, x, x, w, bias.reshape(1, -1))


def _filter_kernel(z_ref, w1_ref, b1_ref, w2_ref, b2_ref, w3_ref, b3_ref, fr_ref, dl_ref,
                   h_ref, nrm_ref, *, tl, width):
    i = pl.program_id(0)
    z = z_ref[...]
    h = jnp.sin(fr_ref[0:1, :] * (jnp.dot(z, w1_ref[...], precision=HI, preferred_element_type=F32)
                                  + b1_ref[...]))
    h = jnp.sin(fr_ref[1:2, :] * (jnp.dot(h, w2_ref[...], precision=HI, preferred_element_type=F32)
                                  + b2_ref[...]))
    h = jnp.dot(h, w3_ref[...], precision=HI, preferred_element_type=F32) + b3_ref[...]
    win = jnp.exp(-z[:, 0:1] * dl_ref[...])
    h = h * jnp.concatenate([win] * (h.shape[1] // width), axis=1)
    h_ref[...] = h
    rows = i * tl + lax.broadcasted_iota(jnp.int32, (tl, 1), 0)
    half = h.shape[1] // 2
    a = jnp.abs(h)
    s_f = jnp.sum(a[:, :half], axis=0, keepdims=True)
    s_b = jnp.sum(jnp.where(rows >= 1, a[:, half:], 0.0), axis=0, keepdims=True)

    @pl.when(i == 0)
    def _():
        nrm_ref[...] = jnp.zeros_like(nrm_ref)

    nrm_ref[...] += s_f + s_b


def _hyena_filter(L, f_w1, f_b1, f_w2, f_b2, f_w3, f_b3, f_freq):
    width = f_w3.shape[1] // 4
    t = jnp.linspace(0.0, 1.0, L, dtype=F32)[:, None]
    w = 2.0 * math.pi * jnp.arange(L, dtype=F32)[:, None] / L
    bands = jnp.linspace(1e-4, HY_EMB_BANDS - 1, HY_EMB_BANDS, dtype=F32)
    z = jnp.concatenate([t, jnp.cos(bands * w), -jnp.sin(bands * w)], axis=-1)
    min_decay = math.log(HY_DECAY_TARGET) / HY_SLOW_DECAY_PCT
    max_decay = math.log(HY_DECAY_TARGET) / HY_FAST_DECAY_PCT
    deltas = jnp.abs(jnp.linspace(min_decay, max_decay, width, dtype=F32))[None, :]
    tl = min(L, 512)
    pe, ph = LANES - z.shape[1], LANES - f_w1.shape[1]
    z = jnp.pad(z, ((0, 0), (0, pe)))
    f_w1 = jnp.pad(f_w1, ((0, pe), (0, ph)))
    f_w2 = jnp.pad(f_w2, ((0, ph), (0, ph)))
    f_w3 = jnp.pad(f_w3, ((0, ph), (0, 0)))
    f_b1, f_b2 = jnp.pad(f_b1, (0, ph)), jnp.pad(f_b2, (0, ph))
    f_freq = jnp.pad(f_freq, ((0, 0), (0, ph)))
    e, hid, n3 = z.shape[1], f_w1.shape[1], f_w3.shape[1]
    full = lambda shape: pl.BlockSpec(shape, lambda i: (0, 0))
    return pl.pallas_call(
        functools.partial(_filter_kernel, tl=tl, width=width),
        out_shape=(jax.ShapeDtypeStruct((L, n3), F32), jax.ShapeDtypeStruct((1, n3 // 2), F32)),
        grid=(L // tl,),
        in_specs=[pl.BlockSpec((tl, e), lambda i: (i, 0)),
                  full((e, hid)), full((1, hid)), full((hid, hid)), full((1, hid)),
                  full((hid, n3)), full((1, n3)), full((2, hid)), full((1, width))],
        out_specs=(pl.BlockSpec((tl, n3), lambda i: (i, 0)), full((1, n3 // 2))),
        compiler_params=_cparams(("arbitrary",)),
        name="hyena_filter",
    )(z, f_w1, f_b1.reshape(1, -1), f_w2, f_b2.reshape(1, -1), f_w3, f_b3.reshape(1, -1), f_freq, deltas)


def _dft_mats(n, sign):
    k = np.arange(n)
    ang = 2.0 * np.pi * ((k[:, None] * k[None, :]) % n) / n
    return np.cos(ang).astype(np.float32), (sign * np.sin(ang)).astype(np.float32)


def _twiddle(n1, n2):
    ang = 2.0 * np.pi * (np.arange(n2)[:, None] * np.arange(n1)[None, :]) / (n1 * n2)
    tr = np.broadcast_to(np.cos(ang)[:, :, None], (n2, n1, LANES)).astype(np.float32)
    ti = np.broadcast_to(-np.sin(ang)[:, :, None], (n2, n1, LANES)).astype(np.float32)
    return tr, ti


def _lane_tile(x, n):
    return x if n == 1 else jnp.concatenate([x] * n, axis=1)


def _fft_s1_kernel(*refs, has_im, tn2, c):
    if has_im:
        xr_ref, xi_ref, fr_ref, fi_ref, tr_ref, ti_ref, or_ref, oi_ref = refs
    else:
        xr_ref, fr_ref, fi_ref, tr_ref, ti_ref, or_ref, oi_ref = refs
    fr, fi = fr_ref[...], fi_ref[...]
    xr = xr_ref[...]
    ar = jnp.dot(fr, xr, precision=HI, preferred_element_type=F32)
    ai = jnp.dot(fi, xr, precision=HI, preferred_element_type=F32)
    if has_im:
        xi = xi_ref[...]
        ar = ar - jnp.dot(fi, xi, precision=HI, preferred_element_type=F32)
        ai = ai + jnp.dot(fr, xi, precision=HI, preferred_element_type=F32)
    for j in range(tn2):
        sl = slice(j * c, (j + 1) * c)
        twr = _lane_tile(tr_ref[j], c // LANES)
        twi = _lane_tile(ti_ref[j], c // LANES)
        or_ref[:, sl] = ar[:, sl] * twr - ai[:, sl] * twi
        oi_ref[:, sl] = ar[:, sl] * twi + ai[:, sl] * twr


def _fft_s1(xr, xi, n1, n2, c, tn2):
    n1in = xr.shape[0]
    fr, fi = _dft_mats(n1, -1.0)
    fr, fi = jnp.asarray(fr[:, :n1in]), jnp.asarray(fi[:, :n1in])
    tr, ti = _twiddle(n1, n2)
    has_im = xi is not None
    xspec = pl.BlockSpec((n1in, tn2 * c), lambda j: (0, j))
    fspec = pl.BlockSpec((n1, n1in), lambda j: (0, 0))
    tspec = pl.BlockSpec((tn2, n1, LANES), lambda j: (j, 0, 0))
    ospec = pl.BlockSpec((n1, tn2 * c), lambda j: (0, j))
    ins = [xr] + ([xi] if has_im else []) + [fr, fi, jnp.asarray(tr), jnp.asarray(ti)]
    return pl.pallas_call(
        functools.partial(_fft_s1_kernel, has_im=has_im, tn2=tn2, c=c),
        out_shape=(jax.ShapeDtypeStruct((n1, n2 * c), F32),) * 2,
        grid=(n2 // tn2,),
        in_specs=[xspec] * (2 if has_im else 1) + [fspec, fspec, tspec, tspec],
        out_specs=(ospec, ospec),
        compiler_params=_cparams(("parallel",)),
        name="fft_stage1",
    )(*ins)


def _fft_is1_kernel(dr_ref, di_ref, gr_ref, gi_ref, tr_ref, ti_ref, yr_ref, yi_ref, *, tn2, c, scale):
    gr, gi = gr_ref[...], gi_ref[...]
    ers, eis = [], []
    for j in range(tn2):
        sl = slice(j * c, (j + 1) * c)
        twr = _lane_tile(tr_ref[j], c // LANES)
        twi = _lane_tile(ti_ref[j], c // LANES)
        dr, di = dr_ref[:, sl], di_ref[:, sl]
        ers.append(dr * twr + di * twi)
        eis.append(di * twr - dr * twi)
    er = jnp.concatenate(ers, axis=1) if tn2 > 1 else ers[0]
    ei = jnp.concatenate(eis, axis=1) if tn2 > 1 else eis[0]
    yr = (jnp.dot(gr, er, precision=HI, preferred_element_type=F32)
          - jnp.dot(gi, ei, precision=HI, preferred_element_type=F32))
    yi = (jnp.dot(gr, ei, precision=HI, preferred_element_type=F32)
          + jnp.dot(gi, er, precision=HI, preferred_element_type=F32))
    yr_ref[...] = yr * scale
    yi_ref[...] = yi * scale


def _fft_is1(dr, di, n1, n2, c, tn2, n1out):
    gr, gi = _dft_mats(n1, +1.0)
    gr, gi = jnp.asarray(gr[:n1out]), jnp.asarray(gi[:n1out])
    tr, ti = _twiddle(n1, n2)
    dspec = pl.BlockSpec((n1, tn2 * c), lambda j: (0, j))
    gspec = pl.BlockSpec((n1out, n1), lambda j: (0, 0))
    tspec = pl.BlockSpec((tn2, n1, LANES), lambda j: (j, 0, 0))
    ospec = pl.BlockSpec((n1out, tn2 * c), lambda j: (0, j))
    return pl.pallas_call(
        functools.partial(_fft_is1_kernel, tn2=tn2, c=c, scale=1.0 / (n1 * n2)),
        out_shape=(jax.ShapeDtypeStruct((n1out, n2 * c), F32),) * 2,
        grid=(n2 // tn2,),
        in_specs=[dspec, dspec, gspec, gspec, tspec, tspec],
        out_specs=(ospec, ospec),
        compiler_params=_cparams(("parallel",)),
        name="fft_inv_stage1",
    )(dr, di, gr, gi, jnp.asarray(tr), jnp.asarray(ti))


def _cdot(fr, fi, xr, xi):
    rr = jnp.dot(fr, xr, precision=HI, preferred_element_type=F32)
    ii = jnp.dot(fi, xi, precision=HI, preferred_element_type=F32)
    ri = jnp.dot(fr, xi, precision=HI, preferred_element_type=F32)
    ir = jnp.dot(fi, xr, precision=HI, preferred_element_type=F32)
    return rr - ii, ri + ir


def _fft_s2_filter_kernel(ar_ref, ai_ref, fr_ref, fi_ref, nrm_ref, kr_ref, ki_ref, *, tk1):
    fr, fi = fr_ref[...], fi_ref[...]
    inv = 1.0 / nrm_ref[...]
    for q in range(tk1):
        br, bi = _cdot(fr, fi, ar_ref[q], ai_ref[q])
        kr_ref[q] = br * inv
        ki_ref[q] = bi * inv


def _fft_s2_filter(ar, ai, nrm, n2, tk1):
    n1, _, c = ar.shape
    fr, fi = _dft_mats(n2, -1.0)
    aspec = pl.BlockSpec((tk1, n2, c), lambda i: (i, 0, 0))
    fspec = pl.BlockSpec((n2, n2), lambda i: (0, 0))
    return pl.pallas_call(
        functools.partial(_fft_s2_filter_kernel, tk1=tk1),
        out_shape=(jax.ShapeDtypeStruct((n1, n2, c), F32),) * 2,
        grid=(n1 // tk1,),
        in_specs=[aspec, aspec, fspec, fspec, pl.BlockSpec((1, c), lambda i: (0, 0))],
        out_specs=(aspec, aspec),
        compiler_params=_cparams(("parallel",)),
        name="fft_stage2_filter",
    )(ar, ai, jnp.asarray(fr), jnp.asarray(fi), nrm)


def _fft_s2_conv_kernel(ar_ref, ai_ref, kr_ref, ki_ref, fr_ref, fi_ref, dr_ref, di_ref, *, tk1):
    fr, fi = fr_ref[...], fi_ref[...]
    for q in range(tk1):
        br, bi = _cdot(fr, fi, ar_ref[q], ai_ref[q])
        kr, ki = kr_ref[q], ki_ref[q]
        pr = br * kr - bi * ki
        pi = br * ki + bi * kr
        dr, di = _cdot(fr, -fi, pr, pi)
        dr_ref[q] = dr
        di_ref[q] = di


def _fft_s2_conv(ar, ai, kr, ki, order, n2, tk1):
    n1, _, c = ar.shape
    fr, fi = _dft_mats(n2, -1.0)
    aspec = pl.BlockSpec((tk1, n2, c), lambda i: (i, 0, 0))
    kspec = pl.BlockSpec((tk1, n2, c), lambda i: (i, 0, order))
    fspec = pl.BlockSpec((n2, n2), lambda i: (0, 0))
    return pl.pallas_call(
        functools.partial(_fft_s2_conv_kernel, tk1=tk1),
        out_shape=(jax.ShapeDtypeStruct((n1, n2, c), F32),) * 2,
        grid=(n1 // tk1,),
        in_specs=[aspec, aspec, kspec, kspec, fspec, fspec],
        out_specs=(aspec, aspec),
        compiler_params=_cparams(("parallel",)),
        name="fft_stage2_conv",
    )(ar, ai, kr, ki, jnp.asarray(fr), jnp.asarray(fi))


def _fft_plan(L, c):
    n = 2 * L
    n1 = 128 if n >= 2048 else 64
    n2 = n // n1
    tn2 = max(1, min(n2, 2048 // c))
    tk1 = 8 if n2 * c * 4 * 8 <= (2 << 20) else max(1, (2 << 20) // (n2 * c * 4))
    return n1, n2, tn2, tk1


def _gate_kernel(y_ref, v_ref, g_ref, b_ref, o_ref):
    o_ref[...] = (y_ref[...] + v_ref[...] * b_ref[...]) * g_ref[...]


def _gate(y, v, g, bias, tm=512):
    b, L, c = y.shape
    tm = min(tm, L)
    spec = pl.BlockSpec((None, tm, c), lambda bi, i: (bi, i, 0))
    return pl.pallas_call(
        _gate_kernel,
        out_shape=jax.ShapeDtypeStruct((b, L, c), F32),
        grid=(b, L // tm),
        in_specs=[spec, spec, spec, pl.BlockSpec((1, c), lambda bi, i: (0, 0))],
        out_specs=spec,
        compiler_params=_cparams(("parallel", "parallel")),
        name="hyena_gate",
    )(y, v, g, bias.reshape(1, c))


def _hyena(u, filt, hy_bias):
    bsz, L, c3 = u.shape
    assert bsz == 2
    c = c3 // 3
    n1, n2, tn2, tk1 = _fft_plan(L, c)
    n1h = n1 // 2
    v, x1, x2 = u[:, :, :c], u[:, :, c:2 * c], u[:, :, 2 * c:]

    hw, nrm = _hyena_filter(L, *filt)
    cf = 2 * c
    kt = jnp.concatenate([hw[:, :cf], jnp.zeros((1, cf), F32), jnp.flip(hw[1:, cf:], axis=0)], axis=0)
    _, _, tn2f, tk1f = _fft_plan(L, cf)
    far, fai = _fft_s1(kt.reshape(n1, n2 * cf), None, n1, n2, cf, tn2f)
    kr, ki = _fft_s2_filter(far.reshape(n1, n2, cf), fai.reshape(n1, n2, cf), nrm, n2, tk1f)

    def long_conv(s, order):
        ar, ai = _fft_s1(s[0].reshape(n1h, n2 * c), s[1].reshape(n1h, n2 * c), n1, n2, c, tn2)
        dr, di = _fft_s2_conv(ar.reshape(n1, n2, c), ai.reshape(n1, n2, c), kr, ki, order, n2, tk1)
        yr, yi = _fft_is1(dr.reshape(n1, n2 * c), di.reshape(n1, n2 * c), n1, n2, c, tn2, n1h)
        return jnp.stack([yr.reshape(L, c), yi.reshape(L, c)])

    z = _gate(long_conv(v, 0), v, x1, hy_bias[0])
    return _gate(long_conv(z, 1), z, x2, hy_bias[1])


def _attn_prep_kernel(q_ref, k_ref, v_ref, c_ref, s1_ref, s2_ref, qs_ref, kr_ref, vt_ref, *, nh, qscale):
    cs, s1, s2 = c_ref[...], s1_ref[...], s2_ref[...]
    lane = lax.broadcasted_iota(jnp.int32, (1, LANES), 1)
    lo = lane < DA_HEAD_DIM

    def rope(x):
        return x * cs + pltpu.roll(x, LANES - 16, 1) * s1 + pltpu.roll(x, 16, 1) * s2

    ones_rows = jnp.ones((8, q_ref.shape[0]), F32)
    for h in range(nh):
        sl = slice(h * LANES, (h + 1) * LANES)
        q = rope(q_ref[:, sl]) * qscale
        qs_ref[h, 0] = jnp.where(lo, q, 0.0).astype(BF16)
        qs_ref[h, 1] = jnp.where(lo, 0.0, q).astype(BF16)
        kr_ref[:, sl] = rope(k_ref[:, sl]).astype(BF16)
        vt_ref[h] = jnp.concatenate([v_ref[:, sl].T, ones_rows], axis=0).astype(BF16)


def _rope_tables(nctx, L):
    rows = L // GRID_W
    r = np.repeat(np.arange(rows), GRID_W).astype(np.float32)
    col = np.tile(np.arange(GRID_W), rows).astype(np.float32)
    axis_dim = DA_HEAD_DIM // 2
    inv = ROPE_THETA ** (-jnp.arange(0, axis_dim, 2, dtype=F32) / axis_dim)
    ar = jnp.asarray(r)[:, None] * inv
    ac = jnp.asarray(col)[:, None] * inv
    cr, sr, cc, sc = jnp.cos(ar), jnp.sin(ar), jnp.cos(ac), jnp.sin(ac)
    z = jnp.zeros_like(sr)
    cos64 = jnp.concatenate([cr, cr, cc, cc], axis=1)
    s1_64 = jnp.concatenate([-sr, z, -sc, z], axis=1)
    s2_64 = jnp.concatenate([z, sr, z, sc], axis=1)
    def full(t64, fill):
        t = jnp.concatenate([t64, t64], axis=1)
        return jnp.concatenate([jnp.full((nctx, LANES), fill, F32), t], axis=0)
    return full(cos64, 1.0), full(s1_64, 0.0), full(s2_64, 0.0)


def _attn_prep(proj, nctx, L, *, qcol, nh, tr=256):
    b, t, _ = proj.shape
    w = nh * LANES
    cs, s1, s2 = _rope_tables(nctx, L)
    qscale = (DA_HEAD_DIM ** -0.5) * math.log2(math.e)
    tspec = pl.BlockSpec((tr, LANES), lambda bi, i: (i, 0))
    return pl.pallas_call(
        functools.partial(_attn_prep_kernel, nh=nh, qscale=qscale),
        out_shape=(jax.ShapeDtypeStruct((b, nh, 2, t, LANES), BF16),
                   jax.ShapeDtypeStruct((b, t, w), BF16),
                   jax.ShapeDtypeStruct((b, nh, LANES + 8, t), BF16)),
        grid=(b, t // tr),
        in_specs=[pl.BlockSpec((None, tr, w), lambda bi, i: (bi, i, qcol)),
                  pl.BlockSpec((None, tr, w), lambda bi, i: (bi, i, qcol + 1)),
                  pl.BlockSpec((None, tr, w), lambda bi, i: (bi, i, qcol + 2)),
                  tspec, tspec, tspec],
        out_specs=(pl.BlockSpec((None, nh, 2, tr, LANES), lambda bi, i: (bi, 0, 0, i, 0)),
                   pl.BlockSpec((None, tr, w), lambda bi, i: (bi, i, 0)),
                   pl.BlockSpec((None, nh, LANES + 8, tr), lambda bi, i: (bi, 0, 0, i))),
        compiler_params=_cparams(("parallel", "parallel")),
        name="attn_prep",
    )(proj, proj, proj, cs, s1, s2)


def _attn_kernel(lam_ref, sw_ref, q_ref, k_ref, vt_ref, o_ref, m_ref, acc_ref, *,
                 tq, nctx, tk, nlat, lam_init):
    i = pl.program_id(2)
    q = q_ref[...].reshape(2 * tq, LANES)
    m_ref[...] = jnp.full_like(m_ref, -1e30)
    acc_ref[...] = jnp.zeros_like(acc_ref)

    def block(kb, vtb):
        s = _nt_dot(kb, q)
        m_old = m_ref[...]
        m_new = jnp.maximum(m_old, jnp.max(s, axis=0, keepdims=True))
        p = jnp.exp2(s - m_new).astype(BF16)
        acc_ref[...] = acc_ref[...] * jnp.exp2(m_old - m_new) + jnp.dot(vtb, p, preferred_element_type=F32)
        m_ref[...] = m_new

    block(k_ref[0:nctx, :], vt_ref[:, 0:nctx])

    def body(j, carry):
        start = pl.multiple_of(nctx + j * tk, LANES)
        block(k_ref[pl.ds(start, tk), :], vt_ref[:, pl.ds(start, tk)])
        return carry

    lax.fori_loop(0, jnp.where(i * tq < nctx, 0, nlat), body, 0)

    acc = acc_ref[...]
    o = acc[0:LANES, :] / acc[LANES:LANES + 1, :]
    lp = lam_ref[...]
    lam = (jnp.exp(jnp.sum(lp[0:1] * lp[1:2], axis=1, keepdims=True))
           - jnp.exp(jnp.sum(lp[2:3] * lp[3:4], axis=1, keepdims=True)) + lam_init)
    a = (o[:, 0:tq] - lam * o[:, tq:2 * tq]).T
    y = a * lax.rsqrt(jnp.mean(a * a, axis=-1, keepdims=True) + EPS) * sw_ref[...]
    o_ref[...] = y * (1.0 - lam_init)


def _diff_attention(qs, kr, vt, lam_p, subln_w, *, nctx, lam_init, tq=256, tk=1024):
    b, nh, _, t, _ = qs.shape
    L = t - nctx
    tk = min(tk, L)
    assert nctx % tq == 0 and L % tk == 0 and nctx % LANES == 0
    return pl.pallas_call(
        functools.partial(_attn_kernel, tq=tq, nctx=nctx, tk=tk, nlat=L // tk, lam_init=lam_init),
        out_shape=jax.ShapeDtypeStruct((b, t, nh * LANES), F32),
        grid=(b, nh, t // tq),
        in_specs=[pl.BlockSpec((4, DA_HEAD_DIM), lambda bi, h, i: (0, 0)),
                  pl.BlockSpec((1, LANES), lambda bi, h, i: (0, 0)),
                  pl.BlockSpec((None, None, 2, tq, LANES), lambda bi, h, i: (bi, h, 0, i, 0)),
                  pl.BlockSpec((None, t, LANES), lambda bi, h, i: (bi, 0, h)),
                  pl.BlockSpec((None, None, LANES + 8, t), lambda bi, h, i: (bi, h, 0, 0))],
        out_specs=pl.BlockSpec((None, tq, LANES), lambda bi, h, i: (bi, i, h)),
        scratch_shapes=[pltpu.VMEM((1, 2 * tq), F32), pltpu.VMEM((LANES + 8, 2 * tq), F32)],
        compiler_params=_cparams(("parallel", "parallel", "arbitrary")),
        name="diff_attention",
    )(lam_p, subln_w.reshape(1, LANES), qs, kr, vt)


def _merge_kernel(hy_ref, hg_ref, at_ref, ag_ref, h_ref, ml_ref, mc_ref, w1_ref, w2_ref, o_ref, *,
                  tm, nctx, d):
    i = pl.program_id(1)
    m1 = (hy_ref[...] * _silu(hg_ref[...])).astype(BF16)
    m2 = (at_ref[...] * _silu(ag_ref[...])).astype(BF16)
    out = (jnp.dot(m1, w1_ref[...], preferred_element_type=F32)
           + jnp.dot(m2, w2_ref[...], preferred_element_type=F32))
    g = _row_select(i, tm, nctx, mc_ref[:, 2 * d:3 * d], ml_ref[:, 2 * d:3 * d])
    o_ref[...] = h_ref[...] + g * out


def _merge_even(hyo, att, proj, h, modl, modc, out_w, *, nctx, hgcol, agcol):
    b, t, d = h.shape
    tm = _row_tile(t)
    c = hyo.shape[2]
    w1, w2 = out_w[:c].astype(BF16), out_w[c:].astype(BF16)
    rspec = lambda col: pl.BlockSpec((None, tm, c), lambda bi, i: (bi, i, col))
    return pl.pallas_call(
        functools.partial(_merge_kernel, tm=tm, nctx=nctx, d=d),
        out_shape=jax.ShapeDtypeStruct((b, t, d), F32),
        grid=(b, t // tm),
        in_specs=[rspec(0), rspec(hgcol), rspec(0), rspec(agcol),
                  pl.BlockSpec((None, tm, d), lambda bi, i: (bi, i, 0)),
                  pl.BlockSpec((None, 1, 3 * d), lambda bi, i: (bi, 0, 0)),
                  pl.BlockSpec((1, 3 * d), lambda bi, i: (0, 0)),
                  pl.BlockSpec((c, d), lambda bi, i: (0, 0)),
                  pl.BlockSpec((att.shape[2], d), lambda bi, i: (0, 0))],
        out_specs=pl.BlockSpec((None, tm, d), lambda bi, i: (bi, i, 0)),
        compiler_params=_cparams(("parallel", "parallel")),
        name="merge_even",
    )(hyo, proj, att, proj, h, modl, modc, w1, w2)


def _softplus(x):
    return jnp.maximum(x, 0.0) + jnp.log(1.0 + jnp.exp(-jnp.abs(x)))


def _ssd_kernel(*refs, reverse, nh, off):
    if reverse:
        x_ref, b_ref, c_ref, dt_ref, yin_ref, dtb_ref, alog_ref, dx_ref, e_ref, y_ref, st_ref = refs
    else:
        x_ref, b_ref, c_ref, dt_ref, dtb_ref, alog_ref, dx_ref, e_ref, y_ref, st_ref = refs
        yin_ref = None
    q = SSD_CHUNK
    p = SSD_HEAD_DIM
    hg = nh // SSD_GROUPS

    @pl.when(pl.program_id(1) == 0)
    def _():
        st_ref[...] = jnp.zeros_like(st_ref)

    ri = lax.broadcasted_iota(jnp.int32, (q, q), 0)
    ci = lax.broadcasted_iota(jnp.int32, (q, q), 1)
    keep = (ci >= ri) if reverse else (ci <= ri)
    tri = jnp.where(keep, 1.0, 0.0).astype(F32)

    dtv = _softplus(dt_ref[...] + dtb_ref[...])
    a = dtv * (-jnp.exp(alog_ref[...]))
    acum = jnp.dot(tri, a, precision=HI, preferred_element_type=F32)
    acum_t = acum.T
    dt_t = dtv.T
    tot = acum[0:1, :] if reverse else acum[q - 1:q, :]
    etot_x = jnp.dot(jnp.broadcast_to(jnp.exp(tot), (8, LANES)), e_ref[...],
                     precision=HI, preferred_element_type=F32)[0:1, :]

    lane = lax.broadcasted_iota(jnp.int32, (1, LANES), 1)
    lo = lane < p
    x = x_ref[...]
    for g in range(SSD_GROUPS):
        gs = slice(g * SSD_STATE, (g + 1) * SSD_STATE)
        bg = b_ref[:, gs]
        cg = c_ref[:, gs]
        cb = _nt_dot(cg.astype(BF16), bg.astype(BF16))
        bg_t = bg.T
        for pr in range(hg * p // LANES):
            ps = slice(g * hg * p + pr * LANES, g * hg * p + (pr + 1) * LANES)
            xp = x[:, ps]
            sp = st_ref[:, ps]
            y_acc = xp * dx_ref[:, ps]
            s_acc = sp * etot_x[:, ps]
            for k in range(LANES // p):
                col = off + g * hg + pr * (LANES // p) + k
                msk = lo if k == 0 else jnp.logical_not(lo)
                ac_col = acum[:, col:col + 1]
                ac_row = acum_t[col:col + 1, :]
                dt_row = dt_t[col:col + 1, :]
                dec = jnp.where(keep, jnp.exp(ac_col - ac_row), 0.0)
                mm = cb * dec * dt_row
                cs = cg * jnp.exp(ac_col)
                lhs = jnp.concatenate([mm, cs], axis=1).astype(BF16)
                rhs = jnp.concatenate([jnp.where(msk, xp, 0.0), jnp.where(msk, sp, 0.0)],
                                      axis=0).astype(BF16)
                y_acc = y_acc + jnp.dot(lhs, rhs, preferred_element_type=F32)
                w_row = dt_row * jnp.exp(tot[:, col:col + 1] - ac_row)
                s_acc = s_acc + jnp.dot((bg_t * w_row).astype(BF16),
                                        jnp.where(msk, xp, 0.0).astype(BF16),
                                        preferred_element_type=F32)
            if reverse:
                y_acc = y_acc + yin_ref[:, ps]
            y_ref[:, ps] = y_acc
            st_ref[:, ps] = s_acc


def _ssd_scan(xbc, dt, dt_bias, a_log, d_skip, y_prev, *, reverse, nctx):
    b, t, _ = xbc.shape
    nh = dt_bias.shape[1]
    inner = nh * SSD_HEAD_DIM
    gn = SSD_GROUPS * SSD_STATE
    q = SSD_CHUNK
    nct, ncc = t // q, nctx // q
    off = nh if reverse else 0
    if reverse:
        cidx = lambda s: jnp.where(s < ncc, ncc - 1 - s, nct - 1 - (s - ncc))
    else:
        cidx = lambda s: s
    pad = lambda v: jnp.pad(v.reshape(1, -1), ((0, 0), (0, LANES - 2 * nh)))
    d_x = jnp.repeat(d_skip[1 if reverse else 0], SSD_HEAD_DIM)[None, :]
    expand = np.zeros((LANES, inner), np.float32)
    for hh in range(nh):
        expand[off + hh, hh * SSD_HEAD_DIM:(hh + 1) * SSD_HEAD_DIM] = 1.0
    assert inner % gn == 0
    xspec = pl.BlockSpec((None, q, inner), lambda bi, s: (bi, cidx(s), 0))
    bspec = pl.BlockSpec((None, q, gn), lambda bi, s: (bi, cidx(s), inner // gn))
    cspec = pl.BlockSpec((None, q, gn), lambda bi, s: (bi, cidx(s), inner // gn + 1))
    dspec = pl.BlockSpec((None, q, LANES), lambda bi, s: (bi, cidx(s), 0))
    vspec = pl.BlockSpec((1, LANES), lambda bi, s: (0, 0))
    ins = [xbc, xbc, xbc, dt] + ([y_prev] if reverse else [])
    specs = [xspec, bspec, cspec, dspec] + ([xspec] if reverse else [])
    ins += [pad(dt_bias), pad(a_log), d_x, jnp.asarray(expand)]
    specs += [vspec, vspec, pl.BlockSpec((1, inner), lambda bi, s: (0, 0)),
              pl.BlockSpec((LANES, inner), lambda bi, s: (0, 0))]
    return pl.pallas_call(
        functools.partial(_ssd_kernel, reverse=reverse, nh=nh, off=off),
        out_shape=jax.ShapeDtypeStruct((b, t, inner), F32),
        grid=(b, nct),
        in_specs=specs,
        out_specs=xspec,
        scratch_shapes=[pltpu.VMEM((SSD_STATE, inner), F32)],
        compiler_params=_cparams(("parallel", "arbitrary")),
        name="ssd_bwd" if reverse else "ssd_fwd",
    )(*ins)


def _finish_kernel(y_ref, z_ref, gw_ref, w_ref, h_ref, ml_ref, fw_ref, o_ref, yn_ref, *, d, gsz):
    y = y_ref[...] * _silu(z_ref[...])
    for g in range(y.shape[1] // gsz):
        sl = slice(g * gsz, (g + 1) * gsz)
        yg = y[:, sl]
        yn_ref[:, sl] = (yg * lax.rsqrt(jnp.mean(yg * yg, axis=-1, keepdims=True) + EPS)
                         * gw_ref[:, sl]).astype(BF16)
    out = jnp.dot(yn_ref[...], w_ref[...], preferred_element_type=F32)
    hn = h_ref[...] + ml_ref[:, 2 * d:3 * d] * out
    o_ref[...] = hn * lax.rsqrt(jnp.mean(hn * hn, axis=-1, keepdims=True) + EPS) * fw_ref[...]


def _finish_odd(y, proj, h, modl, gnorm_w, out_w, final_w, *, nctx, tm=512):
    b, t, d = h.shape
    inner = y.shape[2]
    L = t - nctx
    tm = min(tm, nctx)
    assert nctx % tm == 0 and L % tm == 0
    r0 = nctx // tm
    return pl.pallas_call(
        functools.partial(_finish_kernel, d=d, gsz=inner // SSD_GROUPS),
        out_shape=jax.ShapeDtypeStruct((b, L, d), F32),
        grid=(b, L // tm),
        in_specs=[pl.BlockSpec((None, tm, inner), lambda bi, i: (bi, i + r0, 0)),
                  pl.BlockSpec((None, tm, inner), lambda bi, i: (bi, i + r0, 0)),
                  pl.BlockSpec((1, inner), lambda bi, i: (0, 0)),
                  pl.BlockSpec((inner, d), lambda bi, i: (0, 0)),
                  pl.BlockSpec((None, tm, d), lambda bi, i: (bi, i + r0, 0)),
                  pl.BlockSpec((None, 1, 3 * d), lambda bi, i: (bi, 0, 0)),
                  pl.BlockSpec((1, d), lambda bi, i: (0, 0))],
        out_specs=pl.BlockSpec((None, tm, d), lambda bi, i: (bi, i, 0)),
        scratch_shapes=[pltpu.VMEM((tm, inner), BF16)],
        compiler_params=_cparams(("parallel", "parallel")),
        name="finish_odd",
    )(y, proj, gnorm_w.reshape(1, inner), out_w.astype(BF16), h, modl, final_w.reshape(1, d))


def kernel(x, c, ctx, c_ctx, ada_w, ada_b, norm_w, ev_in_w, ev_out_w, hy_short_w, hy_short_b, hy_f_w1, hy_f_b1, hy_f_w2, hy_f_b2, hy_f_w3, hy_f_b3, hy_freq, hy_bias, da_lambda, da_subln_w, od_in_w, od_conv_w, od_conv_b, od_dt_bias, od_A_log, od_D, od_norm_w, od_out_w, final_norm_w):
    bsz, L, d = x.shape
    nctx = ctx.shape[1]
    depth = ada_w.shape[0]
    assert depth == 2 and bsz == 2
    hyw = hy_bias.shape[2]
    daw = ev_in_w.shape[2] // 4 - hyw
    nh_att = daw // LANES

    h = jnp.concatenate([ctx, x], axis=1)
    cstack = jnp.concatenate([c, c_ctx[None, :], jnp.zeros((8 - bsz - 1, d), F32)], axis=0)
    mod = _modulation(cstack, ada_w, ada_b)

    modl, modc = mod[0, :bsz, None, :], mod[0, bsz:bsz + 1, :]
    proj = _normproj(h, modl, modc, norm_w[0], ev_in_w[0].astype(BF16), nctx=nctx)
    assert hyw == daw
    u = _dwconv3(proj, hy_short_w[0], hy_short_b[0], col0=0, ncol=3, cw=hyw, nctx=nctx, act=False)
    filt = (hy_f_w1[0], hy_f_b1[0], hy_f_w2[0], hy_f_b2[0], hy_f_w3[0], hy_f_b3[0], hy_freq[0])
    hy_l = _hyena(u[:, nctx:], filt, hy_bias[0])
    hy_c = _hyena(u[:, :nctx], filt, hy_bias[0])
    hyo = jnp.concatenate([hy_c, hy_l], axis=1)

    qs, kr, vt = _attn_prep(proj, nctx, L, qcol=4, nh=nh_att)
    lam_init = 0.8 - 0.6 * math.exp(-0.3 * 0)
    att = _diff_attention(qs, kr, vt, da_lambda[0], da_subln_w[0], nctx=nctx, lam_init=lam_init)
    h = _merge_even(hyo, att, proj, h, modl, modc, ev_out_w[0], nctx=nctx, hgcol=3, agcol=7)

    modl, modc = mod[1, :bsz, None, :], mod[1, bsz:bsz + 1, :]
    nh = od_dt_bias.shape[2]
    inner = nh * SSD_HEAD_DIM
    convch = inner + 2 * SSD_GROUPS * SSD_STATE
    w_in = od_in_w[0]
    w_main = w_in[:, :inner + convch].astype(BF16)
    w_dt = jnp.pad(w_in[:, inner + convch:], ((0, 0), (0, LANES - 2 * nh))).astype(BF16)
    proj2 = _normproj(h, modl, modc, norm_w[1], w_main, nctx=nctx)
    dt = _normproj(h, modl, modc, norm_w[1], w_dt, nctx=nctx)
    xbc = _dwconv3(proj2, od_conv_w[0], od_conv_b[0], col0=1, ncol=convch // inner, cw=inner,
                   nctx=nctx, act=True)
    y = _ssd_scan(xbc, dt, od_dt_bias[0], od_A_log[0], od_D[0], None, reverse=False, nctx=nctx)
    y = _ssd_scan(xbc, dt, od_dt_bias[0], od_A_log[0], od_D[0], y, reverse=True, nctx=nctx)
    return _finish_odd(y, proj2, h, modl, od_norm_w[0], od_out_w[0], final_norm_w, nctx=nctx)
```

```python
import functools
import math

import jax
import jax.numpy as jnp
import numpy as np
from jax import lax
from jax.experimental import pallas as pl
from jax.experimental.pallas import tpu as pltpu

F32 = jnp.float32
BF16 = jnp.bfloat16
HI = lax.Precision.HIGHEST

EPS = 1e-6
GRID_W = 64
ROPE_THETA = 10000.0
DA_HEAD_DIM = 64
HY_EMB_BANDS = 16
HY_DECAY_TARGET = 1e-2
HY_FAST_DECAY_PCT = 0.3
HY_SLOW_DECAY_PCT = 1.5
SSD_HEAD_DIM = 64
SSD_GROUPS = 8
SSD_STATE = 128
SSD_CHUNK = 128

LANES = 128
VMEM_LIMIT = 56 * 1024 * 1024


def _row_tile(t, cands=(1280, 1024, 640, 512, 256, 128)):
    return next(c for c in cands if t % c == 0)


def _cparams(sem):
    return pltpu.CompilerParams(dimension_semantics=sem, vmem_limit_bytes=VMEM_LIMIT)


def _silu(x):
    return x * (1.0 / (1.0 + jnp.exp(-x)))


def _nt_dot(a, b):
    return lax.dot_general(a, b, (((1,), (1,)), ((), ())), preferred_element_type=F32)


def _mod_kernel(c_ref, w_ref, b_ref, o_ref):
    s = _silu(c_ref[...])
    o_ref[...] = jnp.dot(s, w_ref[...], precision=HI, preferred_element_type=F32) + b_ref[...]


def _modulation(cstack, ada_w, ada_b):
    depth, d, n = ada_w.shape
    tn = 1024
    return pl.pallas_call(
        _mod_kernel,
        out_shape=jax.ShapeDtypeStruct((depth, 8, n), F32),
        grid=(depth, n // tn),
        in_specs=[pl.BlockSpec((8, d), lambda i, j: (0, 0)),
                  pl.BlockSpec((None, d, tn), lambda i, j: (i, 0, j)),
                  pl.BlockSpec((None, 1, tn), lambda i, j: (i, 0, j))],
        out_specs=pl.BlockSpec((None, 8, tn), lambda i, j: (i, 0, j)),
        compiler_params=_cparams(("parallel", "parallel")),
        name="adaln_mod",
    )(cstack, ada_w, ada_b.reshape(depth, 1, n))


def _row_select(i, tm, nctx, ctx_val, lat_val):
    rows = i * tm + lax.broadcasted_iota(jnp.int32, (tm, 1), 0)
    return jnp.where(rows < nctx, ctx_val, lat_val)


def _normproj_kernel(h_ref, ml_ref, mc_ref, nw_ref, w_ref, o_ref, xn_ref, *, tm, nctx, d):
    i = pl.program_id(1)

    @pl.when(pl.program_id(2) == 0)
    def _():
        x = h_ref[...]
        y = x * lax.rsqrt(jnp.mean(x * x, axis=-1, keepdims=True) + EPS) * nw_ref[...]
        sh = _row_select(i, tm, nctx, mc_ref[:, 0:d], ml_ref[:, 0:d])
        sc = _row_select(i, tm, nctx, mc_ref[:, d:2 * d], ml_ref[:, d:2 * d])
        xn_ref[...] = (y * (1.0 + sc) + sh).astype(BF16)

    o_ref[...] = jnp.dot(xn_ref[...], w_ref[...], preferred_element_type=F32).astype(o_ref.dtype)


def _normproj(h, modl, modc, norm_w, w, *, nctx, tn=1024, out_dtype=F32):
    b, t, d = h.shape
    n = w.shape[1]
    tn = min(tn, n)
    tm = _row_tile(t)
    assert n % tn == 0
    return pl.pallas_call(
        functools.partial(_normproj_kernel, tm=tm, nctx=nctx, d=d),
        out_shape=jax.ShapeDtypeStruct((b, t, n), out_dtype),
        grid=(b, t // tm, n // tn),
        in_specs=[pl.BlockSpec((None, tm, d), lambda bi, i, j: (bi, i, 0)),
                  pl.BlockSpec((None, 1, 3 * d), lambda bi, i, j: (bi, 0, 0)),
                  pl.BlockSpec((1, 3 * d), lambda bi, i, j: (0, 0)),
                  pl.BlockSpec((1, d), lambda bi, i, j: (0, 0)),
                  pl.BlockSpec((d, tn), lambda bi, i, j: (0, j))],
        out_specs=pl.BlockSpec((None, tm, tn), lambda bi, i, j: (bi, i, j)),
        scratch_shapes=[pltpu.VMEM((tm, d), BF16)],
        compiler_params=_cparams(("parallel", "parallel", "arbitrary")),
        name="normproj",
    )(h, modl, modc, norm_w.reshape(1, d), w)


def _dwconv_kernel(x_ref, p_ref, n_ref, w_ref, b_ref, o_ref, *, tm, nctx, t_total, act):
    i = pl.program_id(1)
    x = x_ref[...]
    lrow = lax.broadcasted_iota(jnp.int32, (tm, 1), 0)
    grow = i * tm + lrow
    up = jnp.where(lrow == 0, p_ref[7:8, :], pltpu.roll(x, 1, 0))
    up = jnp.where((grow == 0) | (grow == nctx), 0.0, up)
    dn = jnp.where(lrow == tm - 1, n_ref[0:1, :], pltpu.roll(x, tm - 1, 0))
    dn = jnp.where((grow == nctx - 1) | (grow == t_total - 1), 0.0, dn)
    y = up * w_ref[0:1, :] + x * w_ref[1:2, :] + dn * w_ref[2:3, :] + b_ref[...]
    if act:
        y = _silu(y)
    o_ref[...] = y


def _dwconv3(x, w, bias, *, col0, ncol, cw, nctx, act):
    b, t, _ = x.shape
    tm = _row_tile(t, (640, 512, 256, 128))
    r8 = tm // 8
    nblk8 = t // 8
    return pl.pallas_call(
        functools.partial(_dwconv_kernel, tm=tm, nctx=nctx, t_total=t, act=act),
        out_shape=jax.ShapeDtypeStruct((b, t, ncol * cw), F32),
        grid=(b, t // tm, ncol),
        in_specs=[pl.BlockSpec((None, tm, cw), lambda bi, i, j: (bi, i, col0 + j)),
                  pl.BlockSpec((None, 8, cw), lambda bi, i, j: (bi, jnp.maximum(i * r8 - 1, 0), col0 + j)),
                  pl.BlockSpec((None, 8, cw), lambda bi, i, j: (bi, jnp.minimum((i + 1) * r8, nblk8 - 1), col0 + j)),
                  pl.BlockSpec((3, cw), lambda bi, i, j: (0, j)),
                  pl.BlockSpec((1, cw), lambda bi, i, j: (0, j))],
        out_specs=pl.BlockSpec((None, tm, cw), lambda bi, i, j: (bi, i, j)),
        compiler_params=_cparams(("parallel", "parallel", "parallel")),
        name="dwconv3",
    )(x, x, x, w, bias.reshape(1, -1))


def _filter_kernel(z_ref, w1_ref, b1_ref, w2_ref, b2_ref, w3_ref, b3_ref, fr_ref, dl_ref,
                   h_ref, nrm_ref, *, tl, width, zero_row):
    i = pl.program_id(0)
    z = z_ref[...]
    h = jnp.sin(fr_ref[0:1, :] * (jnp.dot(z, w1_ref[...], precision=HI, preferred_element_type=F32)
                                  + b1_ref[...]))
    h = jnp.sin(fr_ref[1:2, :] * (jnp.dot(h, w2_ref[...], precision=HI, preferred_element_type=F32)
                                  + b2_ref[...]))
    h = jnp.dot(h, w3_ref[...], precision=HI, preferred_element_type=F32) + b3_ref[...]
    win = jnp.exp(-z[:, 0:1] * dl_ref[...])
    h = h * jnp.concatenate([win] * (h.shape[1] // width), axis=1)
    rows = i * tl + lax.broadcasted_iota(jnp.int32, (tl, 1), 0)
    h = jnp.where(rows == zero_row, 0.0, h)
    h_ref[...] = h

    @pl.when(i == 0)
    def _():
        nrm_ref[...] = jnp.zeros_like(nrm_ref)

    nrm_ref[...] += jnp.sum(jnp.abs(h), axis=0, keepdims=True)


def _hyena_filter(L, f_w1, f_b1, f_w2, f_b2, f_w3, f_b3, f_freq):
    width = f_w3.shape[1] // 4
    t = jnp.linspace(0.0, 1.0, L, dtype=F32)[:, None]
    w = 2.0 * math.pi * jnp.arange(L, dtype=F32)[:, None] / L
    bands = jnp.linspace(1e-4, HY_EMB_BANDS - 1, HY_EMB_BANDS, dtype=F32)
    z = jnp.concatenate([t, jnp.cos(bands * w), -jnp.sin(bands * w)], axis=-1)
    z = jnp.concatenate([z, z[:1], jnp.flip(z[1:], axis=0)], axis=0)
    min_decay = math.log(HY_DECAY_TARGET) / HY_SLOW_DECAY_PCT
    max_decay = math.log(HY_DECAY_TARGET) / HY_FAST_DECAY_PCT
    deltas = jnp.abs(jnp.linspace(min_decay, max_decay, width, dtype=F32))[None, :]
    tl = min(L, 512)
    nhalf = L // tl
    pe, ph = LANES - z.shape[1], LANES - f_w1.shape[1]
    z = jnp.pad(z, ((0, 0), (0, pe)))
    f_w1 = jnp.pad(f_w1, ((0, pe), (0, ph)))
    f_w2 = jnp.pad(f_w2, ((0, ph), (0, ph)))
    f_w3 = jnp.pad(f_w3, ((0, ph), (0, 0)))
    f_b1, f_b2 = jnp.pad(f_b1, (0, ph)), jnp.pad(f_b2, (0, ph))
    f_freq = jnp.pad(f_freq, ((0, 0), (0, ph)))
    e, hid, n3 = z.shape[1], f_w1.shape[1], f_w3.shape[1]
    full = lambda shape: pl.BlockSpec(shape, lambda i: (0, 0))
    ncol = n3 // 2
    dirsel = lambda i: (0, jnp.where(i < nhalf, 0, 1))
    return pl.pallas_call(
        functools.partial(_filter_kernel, tl=tl, width=width, zero_row=L),
        out_shape=(jax.ShapeDtypeStruct((2 * L, ncol), F32), jax.ShapeDtypeStruct((1, ncol), F32)),
        grid=(2 * nhalf,),
        in_specs=[pl.BlockSpec((tl, e), lambda i: (i, 0)),
                  full((e, hid)), full((1, hid)), full((hid, hid)), full((1, hid)),
                  pl.BlockSpec((hid, ncol), dirsel), pl.BlockSpec((1, ncol), dirsel),
                  full((2, hid)), full((1, width))],
        out_specs=(pl.BlockSpec((tl, ncol), lambda i: (i, 0)), full((1, ncol))),
        compiler_params=_cparams(("arbitrary",)),
        name="hyena_filter",
    )(z, f_w1, f_b1.reshape(1, -1), f_w2, f_b2.reshape(1, -1), f_w3, f_b3.reshape(1, -1), f_freq, deltas)


def _dft_mats(n, sign):
    k = np.arange(n)
    ang = 2.0 * np.pi * ((k[:, None] * k[None, :]) % n) / n
    return (jnp.asarray(np.cos(ang), dtype=BF16), jnp.asarray(sign * np.sin(ang), dtype=BF16))


def _twiddle(n1, n2, k1_major):
    ang = 2.0 * np.pi * (np.arange(n2)[:, None] * np.arange(n1)[None, :]) / (n1 * n2)
    if k1_major:
        ang = ang.T
    tr = np.broadcast_to(np.cos(ang)[:, :, None], ang.shape + (LANES,)).astype(np.float32)
    ti = np.broadcast_to(-np.sin(ang)[:, :, None], ang.shape + (LANES,)).astype(np.float32)
    return jnp.asarray(tr), jnp.asarray(ti)


def _lane_tile(x, n):
    return x if n == 1 else jnp.concatenate([x] * n, axis=1)


def _fdot(a, b):
    return jnp.dot(a.astype(BF16), b.astype(BF16), preferred_element_type=F32)


def _fft_s1_kernel(*refs, has_im, tn2, c):
    if has_im:
        xr_ref, xi_ref, fr_ref, fi_ref, tr_ref, ti_ref, or_ref, oi_ref = refs
    else:
        xr_ref, fr_ref, fi_ref, tr_ref, ti_ref, or_ref, oi_ref = refs
    fr, fi = fr_ref[...], fi_ref[...]
    xr = xr_ref[...].astype(BF16)
    ar = _fdot(fr, xr)
    ai = _fdot(fi, xr)
    if has_im:
        xi = xi_ref[...].astype(BF16)
        ar = ar - _fdot(fi, xi)
        ai = ai + _fdot(fr, xi)
    for j in range(tn2):
        sl = slice(j * c, (j + 1) * c)
        twr = _lane_tile(tr_ref[j], c // LANES)
        twi = _lane_tile(ti_ref[j], c // LANES)
        or_ref[:, sl] = (ar[:, sl] * twr - ai[:, sl] * twi).astype(or_ref.dtype)
        oi_ref[:, sl] = (ar[:, sl] * twi + ai[:, sl] * twr).astype(oi_ref.dtype)


def _fft_s1(x, n1, n2, c, tn2):
    has_im = x.ndim == 3
    n1in = x.shape[-2]
    fr, fi = _dft_mats(n1, -1.0)
    fr, fi = fr[:, :n1in], fi[:, :n1in]
    tr, ti = _twiddle(n1, n2, False)
    if has_im:
        xspecs = [pl.BlockSpec((None, n1in, tn2 * c), lambda j: (0, 0, j)),
                  pl.BlockSpec((None, n1in, tn2 * c), lambda j: (1, 0, j))]
    else:
        xspecs = [pl.BlockSpec((n1in, tn2 * c), lambda j: (0, j))]
    fspec = pl.BlockSpec((n1, n1in), lambda j: (0, 0))
    tspec = pl.BlockSpec((tn2, n1, LANES), lambda j: (j, 0, 0))
    ospec = pl.BlockSpec((n1, tn2 * c), lambda j: (0, j))
    return pl.pallas_call(
        functools.partial(_fft_s1_kernel, has_im=has_im, tn2=tn2, c=c),
        out_shape=(jax.ShapeDtypeStruct((n1, n2 * c), BF16),) * 2,
        grid=(n2 // tn2,),
        in_specs=xspecs + [fspec, fspec, tspec, tspec],
        out_specs=(ospec, ospec),
        compiler_params=_cparams(("parallel",)),
        name="fft_stage1",
    )(*([x] * len(xspecs)), fr, fi, tr, ti)


def _fft_is1_kernel(er_ref, ei_ref, gr_ref, gi_ref, v_ref, x_ref, b_ref, z_ref, *, tn2, c, scale):
    gr, gi = gr_ref[...], gi_ref[...]
    er, ei = er_ref[...], ei_ref[...]
    yr = (_fdot(gr, er) - _fdot(gi, ei)) * scale
    yi = (_fdot(gr, ei) + _fdot(gi, er)) * scale
    bias = _lane_tile(b_ref[...], tn2)
    z_ref[0] = (yr + v_ref[0] * bias) * x_ref[0]
    z_ref[1] = (yi + v_ref[1] * bias) * x_ref[1]


def _fft_is1_gate(er, ei, v2, x2, bias, n1, n2, c, tn2):
    n1out = n1 // 2
    gr, gi = _dft_mats(n1, +1.0)
    gr, gi = gr[:n1out], gi[:n1out]
    dspec = pl.BlockSpec((n1, tn2 * c), lambda j: (0, j))
    gspec = pl.BlockSpec((n1out, n1), lambda j: (0, 0))
    sspec = pl.BlockSpec((2, n1out, tn2 * c), lambda j: (0, 0, j))
    return pl.pallas_call(
        functools.partial(_fft_is1_kernel, tn2=tn2, c=c, scale=1.0 / (n1 * n2)),
        out_shape=jax.ShapeDtypeStruct((2, n1out, n2 * c), F32),
        grid=(n2 // tn2,),
        in_specs=[dspec, dspec, gspec, gspec, sspec, sspec, pl.BlockSpec((1, c), lambda j: (0, 0))],
        out_specs=sspec,
        compiler_params=_cparams(("parallel",)),
        name="fft_inv_stage1_gate",
    )(er, ei, gr, gi, v2, x2, bias.reshape(1, c))


def _cdot(fr, fi, xr, xi):
    return _fdot(fr, xr) - _fdot(fi, xi), _fdot(fr, xi) + _fdot(fi, xr)


def _fft_s2_filter_kernel(ar_ref, ai_ref, fr_ref, fi_ref, nrm_ref, kr_ref, ki_ref, *, tk1):
    fr, fi = fr_ref[...], fi_ref[...]
    inv = 1.0 / nrm_ref[...]
    for q in range(tk1):
        br, bi = _cdot(fr, fi, ar_ref[q], ai_ref[q])
        kr_ref[q] = br * inv
        ki_ref[q] = bi * inv


def _fft_s2_filter(ar, ai, nrm, n2, tk1):
    n1, _, c = ar.shape
    fr, fi = _dft_mats(n2, -1.0)
    aspec = pl.BlockSpec((tk1, n2, c), lambda i: (i, 0, 0))
    fspec = pl.BlockSpec((n2, n2), lambda i: (0, 0))
    return pl.pallas_call(
        functools.partial(_fft_s2_filter_kernel, tk1=tk1),
        out_shape=(jax.ShapeDtypeStruct((n1, n2, c), F32),) * 2,
        grid=(n1 // tk1,),
        in_specs=[aspec, aspec, fspec, fspec, pl.BlockSpec((1, c), lambda i: (0, 0))],
        out_specs=(aspec, aspec),
        compiler_params=_cparams(("parallel",)),
        name="fft_stage2_filter",
    )(ar, ai, fr, fi, nrm)


def _fft_s2_conv_kernel(ar_ref, ai_ref, kr_ref, ki_ref, fr_ref, fi_ref, tr_ref, ti_ref, er_ref, ei_ref, *,
                        tk1, c):
    fr, fi = fr_ref[...], fi_ref[...]
    nfi = -fi
    for q in range(tk1):
        br, bi = _cdot(fr, fi, ar_ref[q], ai_ref[q])
        kr, ki = kr_ref[q], ki_ref[q]
        pr = br * kr - bi * ki
        pi = br * ki + bi * kr
        dr, di = _cdot(fr, nfi, pr, pi)
        twr = _lane_tile(tr_ref[q], c // LANES)
        twi = _lane_tile(ti_ref[q], c // LANES)
        er_ref[q] = (dr * twr + di * twi).astype(er_ref.dtype)
        ei_ref[q] = (di * twr - dr * twi).astype(ei_ref.dtype)


def _fft_s2_conv(ar, ai, kr, ki, order, n2, tk1):
    n1, _, c = ar.shape
    fr, fi = _dft_mats(n2, -1.0)
    tr, ti = _twiddle(n1, n2, True)
    aspec = pl.BlockSpec((tk1, n2, c), lambda i: (i, 0, 0))
    kspec = pl.BlockSpec((tk1, n2, c), lambda i: (i, 0, order))
    fspec = pl.BlockSpec((n2, n2), lambda i: (0, 0))
    tspec = pl.BlockSpec((tk1, n2, LANES), lambda i: (i, 0, 0))
    return pl.pallas_call(
        functools.partial(_fft_s2_conv_kernel, tk1=tk1, c=c),
        out_shape=(jax.ShapeDtypeStruct((n1, n2, c), BF16),) * 2,
        grid=(n1 // tk1,),
        in_specs=[aspec, aspec, kspec, kspec, fspec, fspec, tspec, tspec],
        out_specs=(aspec, aspec),
        compiler_params=_cparams(("parallel",)),
        name="fft_stage2_conv",
    )(ar, ai, kr, ki, fr, fi, tr, ti)


def _fft_plan(L, c):
    n = 2 * L
    n1 = 128 if n >= 2048 else 64
    n2 = n // n1
    tn2 = max(1, min(n2, 2048 // c))
    tk1 = 8 if n2 * c * 4 * 8 <= (2 << 20) else max(1, (2 << 20) // (n2 * c * 4))
    return n1, n2, tn2, tk1


def _hyena(u, filt, hy_bias):
    bsz, L, c3 = u.shape
    assert bsz == 2
    c = c3 // 3
    n1, n2, tn2, tk1 = _fft_plan(L, c)
    n1h = n1 // 2
    view = lambda a: a.reshape(bsz, n1h, n2 * c)
    v, x1, x2 = view(u[:, :, :c]), view(u[:, :, c:2 * c]), view(u[:, :, 2 * c:])

    kt, nrm = _hyena_filter(L, *filt)
    cf = 2 * c
    _, _, tn2f, tk1f = _fft_plan(L, cf)
    far, fai = _fft_s1(kt.reshape(n1, n2 * cf), n1, n2, cf, tn2f)
    kr, ki = _fft_s2_filter(far.reshape(n1, n2, cf), fai.reshape(n1, n2, cf), nrm, n2, tk1f)

    def conv_gate(s, g, order):
        ar, ai = _fft_s1(s, n1, n2, c, tn2)
        er, ei = _fft_s2_conv(ar.reshape(n1, n2, c), ai.reshape(n1, n2, c), kr, ki, order, n2, tk1)
        return _fft_is1_gate(er.reshape(n1, n2 * c), ei.reshape(n1, n2 * c), s, g, hy_bias[order],
                             n1, n2, c, tn2)

    return conv_gate(conv_gate(v, x1, 0), x2, 1).reshape(bsz, L, c)


def _attn_prep_kernel(q_ref, k_ref, v_ref, c_ref, s1_ref, s2_ref, qs_ref, kr_ref, vt_ref, *, nh, qscale):
    cs, s1, s2 = c_ref[...], s1_ref[...], s2_ref[...]
    lane = lax.broadcasted_iota(jnp.int32, (1, LANES), 1)
    lo = lane < DA_HEAD_DIM

    def rope(x):
        return x * cs + pltpu.roll(x, LANES - 16, 1) * s1 + pltpu.roll(x, 16, 1) * s2

    ones_rows = jnp.ones((8, q_ref.shape[0]), F32)
    for h in range(nh):
        sl = slice(h * LANES, (h + 1) * LANES)
        q = rope(q_ref[:, sl]) * qscale
        qs_ref[h, 0] = jnp.where(lo, q, 0.0).astype(BF16)
        qs_ref[h, 1] = jnp.where(lo, 0.0, q).astype(BF16)
        kr_ref[:, sl] = rope(k_ref[:, sl]).astype(BF16)
        vt_ref[h] = jnp.concatenate([v_ref[:, sl].T, ones_rows], axis=0).astype(BF16)


def _rope_tables(nctx, L):
    rows = L // GRID_W
    r = np.repeat(np.arange(rows), GRID_W).astype(np.float32)
    col = np.tile(np.arange(GRID_W), rows).astype(np.float32)
    axis_dim = DA_HEAD_DIM // 2
    inv = ROPE_THETA ** (-jnp.arange(0, axis_dim, 2, dtype=F32) / axis_dim)
    ar = jnp.asarray(r)[:, None] * inv
    ac = jnp.asarray(col)[:, None] * inv
    cr, sr, cc, sc = jnp.cos(ar), jnp.sin(ar), jnp.cos(ac), jnp.sin(ac)
    z = jnp.zeros_like(sr)
    cos64 = jnp.concatenate([cr, cr, cc, cc], axis=1)
    s1_64 = jnp.concatenate([-sr, z, -sc, z], axis=1)
    s2_64 = jnp.concatenate([z, sr, z, sc], axis=1)
    def full(t64, fill):
        t = jnp.concatenate([t64, t64], axis=1)
        return jnp.concatenate([jnp.full((nctx, LANES), fill, F32), t], axis=0)
    return full(cos64, 1.0), full(s1_64, 0.0), full(s2_64, 0.0)


def _attn_prep(proj, nctx, L, *, qcol, nh, tr=256):
    b, t, _ = proj.shape
    w = nh * LANES
    cs, s1, s2 = _rope_tables(nctx, L)
    qscale = (DA_HEAD_DIM ** -0.5) * math.log2(math.e)
    tspec = pl.BlockSpec((tr, LANES), lambda bi, i: (i, 0))
    return pl.pallas_call(
        functools.partial(_attn_prep_kernel, nh=nh, qscale=qscale),
        out_shape=(jax.ShapeDtypeStruct((b, nh, 2, t, LANES), BF16),
                   jax.ShapeDtypeStruct((b, t, w), BF16),
                   jax.ShapeDtypeStruct((b, nh, LANES + 8, t), BF16)),
        grid=(b, t // tr),
        in_specs=[pl.BlockSpec((None, tr, w), lambda bi, i: (bi, i, qcol)),
                  pl.BlockSpec((None, tr, w), lambda bi, i: (bi, i, qcol + 1)),
                  pl.BlockSpec((None, tr, w), lambda bi, i: (bi, i, qcol + 2)),
                  tspec, tspec, tspec],
        out_specs=(pl.BlockSpec((None, nh, 2, tr, LANES), lambda bi, i: (bi, 0, 0, i, 0)),
                   pl.BlockSpec((None, tr, w), lambda bi, i: (bi, i, 0)),
                   pl.BlockSpec((None, nh, LANES + 8, tr), lambda bi, i: (bi, 0, 0, i))),
        compiler_params=_cparams(("parallel", "parallel")),
        name="attn_prep",
    )(proj, proj, proj, cs, s1, s2)


def _attn_kernel(lam_ref, sw_ref, q_ref, k_ref, vt_ref, o_ref, s0_ref, s1_ref, acc_ref, *,
                 tq, nctx, tk, npairs, lam_init):
    i = pl.program_id(2)
    q = q_ref[...].reshape(2 * tq, LANES)

    def scores(j, s_ref):
        start = pl.multiple_of(j * tk, tk)
        s = _nt_dot(k_ref[pl.ds(start, tk), :], q)
        s_ref[...] = s
        return jnp.max(s, axis=0, keepdims=True)

    def consume(j, s_ref, m, alpha):
        start = pl.multiple_of(j * tk, tk)
        p = jnp.exp2(s_ref[...] - m).astype(BF16)
        acc_ref[...] = acc_ref[...] * alpha + jnp.dot(vt_ref[:, pl.ds(start, tk)], p,
                                                      preferred_element_type=F32)

    @pl.when(i * tq < nctx)
    def _():
        s = _nt_dot(k_ref[0:nctx, :], q)
        p = jnp.exp2(s - jnp.max(s, axis=0, keepdims=True)).astype(BF16)
        acc_ref[...] = jnp.dot(vt_ref[:, 0:nctx], p, preferred_element_type=F32)

    @pl.when(i * tq >= nctx)
    def _():
        acc_ref[...] = jnp.zeros_like(acc_ref)
        m0 = scores(0, s0_ref)

        def pair(pi, carry):
            m, alpha = carry
            m1 = jnp.maximum(m, scores(2 * pi + 1, s1_ref))
            consume(2 * pi, s0_ref, m, alpha)
            m2 = jnp.maximum(m1, scores(2 * pi + 2, s0_ref))
            consume(2 * pi + 1, s1_ref, m1, jnp.exp2(m - m1))
            return m2, jnp.exp2(m1 - m2)

        m, alpha = lax.fori_loop(0, npairs, pair, (m0, jnp.ones_like(m0)))
        consume(2 * npairs, s0_ref, m, alpha)

    acc = acc_ref[...]
    o = acc[0:LANES, :] / acc[LANES:LANES + 1, :]
    lp = lam_ref[...]
    lam = (jnp.exp(jnp.sum(lp[0:1] * lp[1:2], axis=1, keepdims=True))
           - jnp.exp(jnp.sum(lp[2:3] * lp[3:4], axis=1, keepdims=True)) + lam_init)
    a = (o[:, 0:tq] - lam * o[:, tq:2 * tq]).T
    y = a * lax.rsqrt(jnp.mean(a * a, axis=-1, keepdims=True) + EPS) * sw_ref[...]
    o_ref[...] = y * (1.0 - lam_init)


def _diff_attention(qs, kr, vt, lam_p, subln_w, *, nctx, lam_init, tq=256):
    b, nh, _, t, _ = qs.shape
    tk = next(c for c in (1280, 1024, 768, 640, 512, 384, 256, 128) if t % c == 0 and (t // c) % 2 == 1)
    assert nctx % tq == 0 and nctx % LANES == 0
    return pl.pallas_call(
        functools.partial(_attn_kernel, tq=tq, nctx=nctx, tk=tk, npairs=(t // tk) // 2, lam_init=lam_init),
        out_shape=jax.ShapeDtypeStruct((b, t, nh * LANES), F32),
        grid=(b, nh, t // tq),
        in_specs=[pl.BlockSpec((4, DA_HEAD_DIM), lambda bi, h, i: (0, 0)),
                  pl.BlockSpec((1, LANES), lambda bi, h, i: (0, 0)),
                  pl.BlockSpec((None, None, 2, tq, LANES), lambda bi, h, i: (bi, h, 0, i, 0)),
                  pl.BlockSpec((None, t, LANES), lambda bi, h, i: (bi, 0, h)),
                  pl.BlockSpec((None, None, LANES + 8, t), lambda bi, h, i: (bi, h, 0, 0))],
        out_specs=pl.BlockSpec((None, tq, LANES), lambda bi, h, i: (bi, i, h)),
        scratch_shapes=[pltpu.VMEM((tk, 2 * tq), F32), pltpu.VMEM((tk, 2 * tq), F32),
                        pltpu.VMEM((LANES + 8, 2 * tq), F32)],
        compiler_params=_cparams(("parallel", "parallel", "arbitrary")),
        name="diff_attention",
    )(lam_p, subln_w.reshape(1, LANES), qs, kr, vt)


def _merge_kernel(hy_ref, hg_ref, at_ref, ag_ref, h_ref, ml_ref, mc_ref, w1_ref, w2_ref, o_ref, *,
                  tm, nctx, d):
    i = pl.program_id(1)
    m1 = (hy_ref[...] * _silu(hg_ref[...])).astype(BF16)
    m2 = (at_ref[...] * _silu(ag_ref[...])).astype(BF16)
    out = (jnp.dot(m1, w1_ref[...], preferred_element_type=F32)
           + jnp.dot(m2, w2_ref[...], preferred_element_type=F32))
    g = _row_select(i, tm, nctx, mc_ref[:, 2 * d:3 * d], ml_ref[:, 2 * d:3 * d])
    o_ref[...] = h_ref[...] + g * out


def _merge_even(hyo, att, proj, h, modl, modc, out_w, *, nctx, hgcol, agcol):
    b, t, d = h.shape
    tm = _row_tile(t)
    c = hyo.shape[2]
    w1, w2 = out_w[:c].astype(BF16), out_w[c:].astype(BF16)
    rspec = lambda col: pl.BlockSpec((None, tm, c), lambda bi, i: (bi, i, col))
    return pl.pallas_call(
        functools.partial(_merge_kernel, tm=tm, nctx=nctx, d=d),
        out_shape=jax.ShapeDtypeStruct((b, t, d), F32),
        grid=(b, t // tm),
        in_specs=[rspec(0), rspec(hgcol), rspec(0), rspec(agcol),
                  pl.BlockSpec((None, tm, d), lambda bi, i: (bi, i, 0)),
                  pl.BlockSpec((None, 1, 3 * d), lambda bi, i: (bi, 0, 0)),
                  pl.BlockSpec((1, 3 * d), lambda bi, i: (0, 0)),
                  pl.BlockSpec((c, d), lambda bi, i: (0, 0)),
                  pl.BlockSpec((att.shape[2], d), lambda bi, i: (0, 0))],
        out_specs=pl.BlockSpec((None, tm, d), lambda bi, i: (bi, i, 0)),
        compiler_params=_cparams(("parallel", "parallel")),
        name="merge_even",
    )(hyo, proj, att, proj, h, modl, modc, w1, w2)


def _softplus(x):
    return jnp.maximum(x, 0.0) + jnp.log(1.0 + jnp.exp(-jnp.abs(x)))


def _ssd_kernel(*refs, reverse, nh, off):
    if reverse:
        x_ref, b_ref, c_ref, dt_ref, yin_ref, dtb_ref, alog_ref, dx_ref, e_ref, y_ref, st_ref = refs
    else:
        x_ref, b_ref, c_ref, dt_ref, dtb_ref, alog_ref, dx_ref, e_ref, y_ref, st_ref = refs
        yin_ref = None
    q = SSD_CHUNK
    p = SSD_HEAD_DIM
    hg = nh // SSD_GROUPS

    @pl.when(pl.program_id(1) == 0)
    def _():
        st_ref[...] = jnp.zeros_like(st_ref)

    ri = lax.broadcasted_iota(jnp.int32, (q, q), 0)
    ci = lax.broadcasted_iota(jnp.int32, (q, q), 1)
    keep = (ci >= ri) if reverse else (ci <= ri)
    tri = jnp.where(keep, 1.0, 0.0).astype(F32)

    dtv = _softplus(dt_ref[...] + dtb_ref[...])
    a = dtv * (-jnp.exp(alog_ref[...]))
    acum = jnp.dot(tri, a, precision=HI, preferred_element_type=F32)
    acum_t = acum.T
    dt_t = dtv.T
    tot = acum[0:1, :] if reverse else acum[q - 1:q, :]
    etot_x = jnp.dot(jnp.broadcast_to(jnp.exp(tot), (8, LANES)), e_ref[...],
                     precision=HI, preferred_element_type=F32)[0:1, :]

    lane = lax.broadcasted_iota(jnp.int32, (1, LANES), 1)
    lo = lane < p
    x = x_ref[...]
    for g in range(SSD_GROUPS):
        gs = slice(g * SSD_STATE, (g + 1) * SSD_STATE)
        bg = b_ref[:, gs]
        cg = c_ref[:, gs]
        cb = _nt_dot(cg.astype(BF16), bg.astype(BF16))
        bg_t = bg.T
        for pr in range(hg * p // LANES):
            ps = slice(g * hg * p + pr * LANES, g * hg * p + (pr + 1) * LANES)
            xp = x[:, ps]
            sp = st_ref[:, ps]
            y_acc = xp * dx_ref[:, ps]
            s_acc = sp * etot_x[:, ps]
            for k in range(LANES // p):
                col = off + g * hg + pr * (LANES // p) + k
                msk = lo if k == 0 else jnp.logical_not(lo)
                ac_col = acum[:, col:col + 1]
                ac_row = acum_t[col:col + 1, :]
                dt_row = dt_t[col:col + 1, :]
                dec = jnp.where(keep, jnp.exp(ac_col - ac_row), 0.0)
                mm = cb * dec * dt_row
                cs = cg * jnp.exp(ac_col)
                lhs = jnp.concatenate([mm, cs], axis=1).astype(BF16)
                rhs = jnp.concatenate([jnp.where(msk, xp, 0.0), jnp.where(msk, sp, 0.0)],
                                      axis=0).astype(BF16)
                y_acc = y_acc + jnp.dot(lhs, rhs, preferred_element_type=F32)
                w_row = dt_row * jnp.exp(tot[:, col:col + 1] - ac_row)
                s_acc = s_acc + jnp.dot((bg_t * w_row).astype(BF16),
                                        jnp.where(msk, xp, 0.0).astype(BF16),
                                        preferred_element_type=F32)
            if reverse:
                y_acc = y_acc + yin_ref[:, ps]
            y_ref[:, ps] = y_acc
            st_ref[:, ps] = s_acc


def _ssd_scan(xbc, dt, dt_bias, a_log, d_skip, y_prev, *, reverse, nctx):
    b, t, _ = xbc.shape
    nh = dt_bias.shape[1]
    inner = nh * SSD_HEAD_DIM
    gn = SSD_GROUPS * SSD_STATE
    q = SSD_CHUNK
    nct, ncc = t // q, nctx // q
    off = nh if reverse else 0
    if reverse:
        cidx = lambda s: jnp.where(s < ncc, ncc - 1 - s, nct - 1 - (s - ncc))
    else:
        cidx = lambda s: s
    pad = lambda v: jnp.pad(v.reshape(1, -1), ((0, 0), (0, LANES - 2 * nh)))
    d_x = jnp.repeat(d_skip[1 if reverse else 0], SSD_HEAD_DIM)[None, :]
    expand = np.zeros((LANES, inner), np.float32)
    for hh in range(nh):
        expand[off + hh, hh * SSD_HEAD_DIM:(hh + 1) * SSD_HEAD_DIM] = 1.0
    assert inner % gn == 0
    xspec = pl.BlockSpec((None, q, inner), lambda bi, s: (bi, cidx(s), 0))
    bspec = pl.BlockSpec((None, q, gn), lambda bi, s: (bi, cidx(s), inner // gn))
    cspec = pl.BlockSpec((None, q, gn), lambda bi, s: (bi, cidx(s), inner // gn + 1))
    dspec = pl.BlockSpec((None, q, LANES), lambda bi, s: (bi, cidx(s), 0))
    vspec = pl.BlockSpec((1, LANES), lambda bi, s: (0, 0))
    ins = [xbc, xbc, xbc, dt] + ([y_prev] if reverse else [])
    specs = [xspec, bspec, cspec, dspec] + ([xspec] if reverse else [])
    ins += [pad(dt_bias), pad(a_log), d_x, jnp.asarray(expand)]
    specs += [vspec, vspec, pl.BlockSpec((1, inner), lambda bi, s: (0, 0)),
              pl.BlockSpec((LANES, inner), lambda bi, s: (0, 0))]
    return pl.pallas_call(
        functools.partial(_ssd_kernel, reverse=reverse, nh=nh, off=off),
        out_shape=jax.ShapeDtypeStruct((b, t, inner), F32),
        grid=(b, nct),
        in_specs=specs,
        out_specs=xspec,
        scratch_shapes=[pltpu.VMEM((SSD_STATE, inner), F32)],
        compiler_params=_cparams(("parallel", "arbitrary")),
        name="ssd_bwd" if reverse else "ssd_fwd",
    )(*ins)


def _finish_kernel(y_ref, z_ref, gw_ref, w_ref, h_ref, ml_ref, fw_ref, o_ref, yn_ref, *, d, gsz):
    y = y_ref[...] * _silu(z_ref[...])
    for g in range(y.shape[1] // gsz):
        sl = slice(g * gsz, (g + 1) * gsz)
        yg = y[:, sl]
        yn_ref[:, sl] = (yg * lax.rsqrt(jnp.mean(yg * yg, axis=-1, keepdims=True) + EPS)
                         * gw_ref[:, sl]).astype(BF16)
    out = jnp.dot(yn_ref[...], w_ref[...], preferred_element_type=F32)
    hn = h_ref[...] + ml_ref[:, 2 * d:3 * d] * out
    o_ref[...] = hn * lax.rsqrt(jnp.mean(hn * hn, axis=-1, keepdims=True) + EPS) * fw_ref[...]


def _finish_odd(y, proj, h, modl, gnorm_w, out_w, final_w, *, nctx, tm=512):
    b, t, d = h.shape
    inner = y.shape[2]
    L = t - nctx
    tm = min(tm, nctx)
    assert nctx % tm == 0 and L % tm == 0
    r0 = nctx // tm
    return pl.pallas_call(
        functools.partial(_finish_kernel, d=d, gsz=inner // SSD_GROUPS),
        out_shape=jax.ShapeDtypeStruct((b, L, d), F32),
        grid=(b, L // tm),
        in_specs=[pl.BlockSpec((None, tm, inner), lambda bi, i: (bi, i + r0, 0)),
                  pl.BlockSpec((None, tm, inner), lambda bi, i: (bi, i + r0, 0)),
                  pl.BlockSpec((1, inner), lambda bi, i: (0, 0)),
                  pl.BlockSpec((inner, d), lambda bi, i: (0, 0)),
                  pl.BlockSpec((None, tm, d), lambda bi, i: (bi, i + r0, 0)),
                  pl.BlockSpec((None, 1, 3 * d), lambda bi, i: (bi, 0, 0)),
                  pl.BlockSpec((1, d), lambda bi, i: (0, 0))],
        out_specs=pl.BlockSpec((None, tm, d), lambda bi, i: (bi, i, 0)),
        scratch_shapes=[pltpu.VMEM((tm, inner), BF16)],
        compiler_params=_cparams(("parallel", "parallel")),
        name="finish_odd",
    )(y, proj, gnorm_w.reshape(1, inner), out_w.astype(BF16), h, modl, final_w.reshape(1, d))


def kernel(x, c, ctx, c_ctx, ada_w, ada_b, norm_w, ev_in_w, ev_out_w, hy_short_w, hy_short_b, hy_f_w1, hy_f_b1, hy_f_w2, hy_f_b2, hy_f_w3, hy_f_b3, hy_freq, hy_bias, da_lambda, da_subln_w, od_in_w, od_conv_w, od_conv_b, od_dt_bias, od_A_log, od_D, od_norm_w, od_out_w, final_norm_w):
    bsz, L, d = x.shape
    nctx = ctx.shape[1]
    depth = ada_w.shape[0]
    assert depth == 2 and bsz == 2
    hyw = hy_bias.shape[2]
    daw = ev_in_w.shape[2] // 4 - hyw
    nh_att = daw // LANES

    h = jnp.concatenate([ctx, x], axis=1)
    cstack = jnp.concatenate([c, c_ctx[None, :], jnp.zeros((8 - bsz - 1, d), F32)], axis=0)
    mod = _modulation(cstack, ada_w, ada_b)

    modl, modc = mod[0, :bsz, None, :], mod[0, bsz:bsz + 1, :]
    proj = _normproj(h, modl, modc, norm_w[0], ev_in_w[0].astype(BF16), nctx=nctx)
    assert hyw == daw
    u = _dwconv3(proj, hy_short_w[0], hy_short_b[0], col0=0, ncol=3, cw=hyw, nctx=nctx, act=False)
    filt = (hy_f_w1[0], hy_f_b1[0], hy_f_w2[0], hy_f_b2[0], hy_f_w3[0], hy_f_b3[0], hy_freq[0])
    hy_l = _hyena(u[:, nctx:], filt, hy_bias[0])
    hy_c = _hyena(u[:, :nctx], filt, hy_bias[0])
    hyo = jnp.concatenate([hy_c, hy_l], axis=1)

    qs, kr, vt = _attn_prep(proj, nctx, L, qcol=4, nh=nh_att)
    lam_init = 0.8 - 0.6 * math.exp(-0.3 * 0)
    att = _diff_attention(qs, kr, vt, da_lambda[0], da_subln_w[0], nctx=nctx, lam_init=lam_init)
    h = _merge_even(hyo, att, proj, h, modl, modc, ev_out_w[0], nctx=nctx, hgcol=3, agcol=7)

    modl, modc = mod[1, :bsz, None, :], mod[1, bsz:bsz + 1, :]
    nh = od_dt_bias.shape[2]
    inner = nh * SSD_HEAD_DIM
    convch = inner + 2 * SSD_GROUPS * SSD_STATE
    w_in = od_in_w[0]
    w_main = w_in[:, :inner + convch].astype(BF16)
    w_dt = jnp.pad(w_in[:, inner + convch:], ((0, 0), (0, LANES - 2 * nh))).astype(BF16)
    proj2 = _normproj(h, modl, modc, norm_w[1], w_main, nctx=nctx)
    dt = _normproj(h, modl, modc, norm_w[1], w_dt, nctx=nctx)
    xbc = _dwconv3(proj2, od_conv_w[0], od_conv_b[0], col0=1, ncol=convch // inner, cw=inner,
                   nctx=nctx, act=True)
    y = _ssd_scan(xbc, dt, od_dt_bias[0], od_A_log[0], od_D[0], None, reverse=False, nctx=nctx)
    y = _ssd_scan(xbc, dt, od_dt_bias[0], od_A_log[0], od_D[0], y, reverse=True, nctx=nctx)
    return _finish_odd(y, proj2, h, modl, od_norm_w[0], od_out_w[0], final_norm_w, nctx=nctx)
```

```python
import functools
import math

import jax
import jax.numpy as jnp
import numpy as np
from jax import lax
from jax.experimental import pallas as pl
from jax.experimental.pallas import tpu as pltpu

F32 = jnp.float32
BF16 = jnp.bfloat16
HI = lax.Precision.HIGHEST

EPS = 1e-6
GRID_W = 64
ROPE_THETA = 10000.0
DA_HEAD_DIM = 64
HY_EMB_BANDS = 16
HY_DECAY_TARGET = 1e-2
HY_FAST_DECAY_PCT = 0.3
HY_SLOW_DECAY_PCT = 1.5
SSD_HEAD_DIM = 64
SSD_GROUPS = 8
SSD_STATE = 128
SSD_CHUNK = 128

LANES = 128
HALO = 16
VMEM_LIMIT = 56 * 1024 * 1024


def _row_tile(t, cands=(1280, 1024, 640, 512, 256, 128)):
    return next(c for c in cands if t % c == 0)


def _cparams(sem):
    return pltpu.CompilerParams(dimension_semantics=sem, vmem_limit_bytes=VMEM_LIMIT)


def _silu(x):
    return x * (1.0 / (1.0 + jnp.exp(-x)))


def _nt_dot(a, b):
    return lax.dot_general(a, b, (((1,), (1,)), ((), ())), preferred_element_type=F32)


def _mod_kernel(c_ref, w_ref, b_ref, o_ref):
    s = _silu(c_ref[...])
    o_ref[...] = jnp.dot(s, w_ref[...], precision=HI, preferred_element_type=F32) + b_ref[...]


def _modulation(cstack, ada_w, ada_b):
    depth, d, n = ada_w.shape
    tn = 1024
    return pl.pallas_call(
        _mod_kernel,
        out_shape=jax.ShapeDtypeStruct((depth, 8, n), F32),
        grid=(depth, n // tn),
        in_specs=[pl.BlockSpec((8, d), lambda i, j: (0, 0)),
                  pl.BlockSpec((None, d, tn), lambda i, j: (i, 0, j)),
                  pl.BlockSpec((None, 1, tn), lambda i, j: (i, 0, j))],
        out_specs=pl.BlockSpec((None, 8, tn), lambda i, j: (i, 0, j)),
        compiler_params=_cparams(("parallel", "parallel")),
        name="adaln_mod",
    )(cstack, ada_w, ada_b.reshape(depth, 1, n))


def _row_select(i, tm, nctx, ctx_val, lat_val):
    rows = i * tm + lax.broadcasted_iota(jnp.int32, (tm, 1), 0)
    return jnp.where(rows < nctx, ctx_val, lat_val)


def _normproj_kernel(h_ref, ml_ref, mc_ref, nw_ref, w_ref, o_ref, xn_ref, *, tm, nctx, d):
    i = pl.program_id(1)

    @pl.when(pl.program_id(2) == 0)
    def _():
        x = h_ref[...]
        y = x * lax.rsqrt(jnp.mean(x * x, axis=-1, keepdims=True) + EPS) * nw_ref[...]
        sh = _row_select(i, tm, nctx, mc_ref[:, 0:d], ml_ref[:, 0:d])
        sc = _row_select(i, tm, nctx, mc_ref[:, d:2 * d], ml_ref[:, d:2 * d])
        xn_ref[...] = (y * (1.0 + sc) + sh).astype(BF16)

    o_ref[...] = jnp.dot(xn_ref[...], w_ref[...], preferred_element_type=F32).astype(o_ref.dtype)


def _normproj(h, modl, modc, norm_w, w, *, nctx, tn=1024, out_dtype=F32):
    b, t, d = h.shape
    n = w.shape[1]
    tn = min(tn, n)
    tm = _row_tile(t)
    assert n % tn == 0
    return pl.pallas_call(
        functools.partial(_normproj_kernel, tm=tm, nctx=nctx, d=d),
        out_shape=jax.ShapeDtypeStruct((b, t, n), out_dtype),
        grid=(b, t // tm, n // tn),
        in_specs=[pl.BlockSpec((None, tm, d), lambda bi, i, j: (bi, i, 0)),
                  pl.BlockSpec((None, 1, 3 * d), lambda bi, i, j: (bi, 0, 0)),
                  pl.BlockSpec((1, 3 * d), lambda bi, i, j: (0, 0)),
                  pl.BlockSpec((1, d), lambda bi, i, j: (0, 0)),
                  pl.BlockSpec((d, tn), lambda bi, i, j: (0, j))],
        out_specs=pl.BlockSpec((None, tm, tn), lambda bi, i, j: (bi, i, j)),
        scratch_shapes=[pltpu.VMEM((tm, d), BF16)],
        compiler_params=_cparams(("parallel", "parallel", "arbitrary")),
        name="normproj",
    )(h, modl, modc, norm_w.reshape(1, d), w)


def _dwconv_kernel(x_ref, p_ref, n_ref, w_ref, b_ref, o_ref, *, tm, nctx, t_total, act):
    i = pl.program_id(1)
    x = x_ref[...].astype(F32)
    lrow = lax.broadcasted_iota(jnp.int32, (tm, 1), 0)
    grow = i * tm + lrow
    up = jnp.where(lrow == 0, p_ref[HALO - 1:HALO, :].astype(F32), pltpu.roll(x, 1, 0))
    up = jnp.where((grow == 0) | (grow == nctx), 0.0, up)
    dn = jnp.where(lrow == tm - 1, n_ref[0:1, :].astype(F32), pltpu.roll(x, tm - 1, 0))
    dn = jnp.where((grow == nctx - 1) | (grow == t_total - 1), 0.0, dn)
    y = up * w_ref[0:1, :] + x * w_ref[1:2, :] + dn * w_ref[2:3, :] + b_ref[...]
    if act:
        y = _silu(y)
    o_ref[...] = y.astype(o_ref.dtype)


def _dwconv3(x, w, bias, *, col0, ncol, cw, nctx, act):
    b, t, _ = x.shape
    tm = _row_tile(t, (640, 512, 256, 128))
    r8 = tm // HALO
    nblk8 = t // HALO
    return pl.pallas_call(
        functools.partial(_dwconv_kernel, tm=tm, nctx=nctx, t_total=t, act=act),
        out_shape=jax.ShapeDtypeStruct((b, t, ncol * cw), BF16),
        grid=(b, t // tm, ncol),
        in_specs=[pl.BlockSpec((None, tm, cw), lambda bi, i, j: (bi, i, col0 + j)),
                  pl.BlockSpec((None, HALO, cw), lambda bi, i, j: (bi, jnp.maximum(i * r8 - 1, 0), col0 + j)),
                  pl.BlockSpec((None, HALO, cw),
                               lambda bi, i, j: (bi, jnp.minimum((i + 1) * r8, nblk8 - 1), col0 + j)),
                  pl.BlockSpec((3, cw), lambda bi, i, j: (0, j)),
                  pl.BlockSpec((1, cw), lambda bi, i, j: (0, j))],
        out_specs=pl.BlockSpec((None, tm, cw), lambda bi, i, j: (bi, i, j)),
        compiler_params=_cparams(("parallel", "parallel", "parallel")),
        name="dwconv3",
    )(x, x, x, w, bias.reshape(1, -1))


def _filter_kernel(z_ref, w1_ref, b1_ref, w2_ref, b2_ref, w3h_ref, w3l_ref, b3_ref, fr_ref, dl_ref,
                   h_ref, nrm_ref, *, tl, width, zero_row):
    i = pl.program_id(0)
    z = z_ref[...]
    h = jnp.sin(fr_ref[0:1, :] * (jnp.dot(z, w1_ref[...], precision=HI, preferred_element_type=F32)
                                  + b1_ref[...]))
    h = jnp.sin(fr_ref[1:2, :] * (jnp.dot(h, w2_ref[...], precision=HI, preferred_element_type=F32)
                                  + b2_ref[...]))
    hh = h.astype(BF16)
    hl = (h - hh.astype(F32)).astype(BF16)
    w3h = w3h_ref[...]
    h = (jnp.dot(hh, w3h, preferred_element_type=F32) + jnp.dot(hl, w3h, preferred_element_type=F32)
         + jnp.dot(hh, w3l_ref[...], preferred_element_type=F32)) + b3_ref[...]
    win = jnp.exp(-z[:, 0:1] * dl_ref[...])
    h = h * jnp.concatenate([win] * (h.shape[1] // width), axis=1)
    rows = i * tl + lax.broadcasted_iota(jnp.int32, (tl, 1), 0)
    h = jnp.where(rows == zero_row, 0.0, h)
    h_ref[...] = h.astype(h_ref.dtype)

    @pl.when(i == 0)
    def _():
        nrm_ref[...] = jnp.zeros_like(nrm_ref)

    nrm_ref[...] += jnp.sum(jnp.abs(h), axis=0, keepdims=True)


def _hyena_filter(L, f_w1, f_b1, f_w2, f_b2, f_w3, f_b3, f_freq):
    width = f_w3.shape[1] // 4
    f32 = np.float32
    t = np.linspace(0.0, 1.0, L, dtype=f32)[:, None]
    w = f32(2.0 * math.pi) * np.arange(L, dtype=f32)[:, None] / f32(L)
    bands = np.linspace(1e-4, HY_EMB_BANDS - 1, HY_EMB_BANDS, dtype=f32)
    z = np.concatenate([t, np.cos(bands * w), -np.sin(bands * w)], axis=-1).astype(f32)
    z = np.concatenate([z, z[:1], z[:0:-1]], axis=0)
    min_decay = math.log(HY_DECAY_TARGET) / HY_SLOW_DECAY_PCT
    max_decay = math.log(HY_DECAY_TARGET) / HY_FAST_DECAY_PCT
    deltas = jnp.asarray(np.abs(np.linspace(min_decay, max_decay, width, dtype=f32))[None, :])
    tl = min(L, 512)
    nhalf = L // tl
    pe, ph = LANES - z.shape[1], LANES - f_w1.shape[1]
    z = jnp.asarray(np.pad(z, ((0, 0), (0, pe))))
    f_w1 = jnp.pad(f_w1, ((0, pe), (0, ph)))
    f_w2 = jnp.pad(f_w2, ((0, ph), (0, ph)))
    f_w3 = jnp.pad(f_w3, ((0, ph), (0, 0)))
    f_b1, f_b2 = jnp.pad(f_b1, (0, ph)), jnp.pad(f_b2, (0, ph))
    f_freq = jnp.pad(f_freq, ((0, 0), (0, ph)))
    e, hid, n3 = z.shape[1], f_w1.shape[1], f_w3.shape[1]
    full = lambda shape: pl.BlockSpec(shape, lambda i: (0, 0))
    ncol = n3 // 2
    dirsel = lambda i: (0, jnp.where(i < nhalf, 0, 1))
    w3h = f_w3.astype(BF16)
    w3l = (f_w3 - w3h.astype(F32)).astype(BF16)
    return pl.pallas_call(
        functools.partial(_filter_kernel, tl=tl, width=width, zero_row=L),
        out_shape=(jax.ShapeDtypeStruct((2 * L, ncol), BF16), jax.ShapeDtypeStruct((1, ncol), F32)),
        grid=(2 * nhalf,),
        in_specs=[pl.BlockSpec((tl, e), lambda i: (i, 0)),
                  full((e, hid)), full((1, hid)), full((hid, hid)), full((1, hid)),
                  pl.BlockSpec((hid, ncol), dirsel), pl.BlockSpec((hid, ncol), dirsel),
                  pl.BlockSpec((1, ncol), dirsel), full((2, hid)), full((1, width))],
        out_specs=(pl.BlockSpec((tl, ncol), lambda i: (i, 0)), full((1, ncol))),
        compiler_params=_cparams(("arbitrary",)),
        name="hyena_filter",
    )(z, f_w1, f_b1.reshape(1, -1), f_w2, f_b2.reshape(1, -1), w3h, w3l, f_b3.reshape(1, -1), f_freq, deltas)


def _dft_mats(n, sign):
    k = np.arange(n)
    ang = 2.0 * np.pi * ((k[:, None] * k[None, :]) % n) / n
    return (jnp.asarray(np.cos(ang), dtype=BF16), jnp.asarray(sign * np.sin(ang), dtype=BF16))


def _twiddle(n1, n2, k1_major):
    ang = 2.0 * np.pi * (np.arange(n2)[:, None] * np.arange(n1)[None, :]) / (n1 * n2)
    if k1_major:
        ang = ang.T
    tr = np.broadcast_to(np.cos(ang)[:, :, None], ang.shape + (LANES,)).astype(np.float32)
    ti = np.broadcast_to(-np.sin(ang)[:, :, None], ang.shape + (LANES,)).astype(np.float32)
    return jnp.asarray(tr), jnp.asarray(ti)


def _lane_tile(x, n):
    return x if n == 1 else jnp.concatenate([x] * n, axis=1)


def _fdot(a, b):
    return jnp.dot(a.astype(BF16), b.astype(BF16), preferred_element_type=F32)


def _fft_s1_kernel(*refs, has_im, tn2, c):
    if has_im:
        xr_ref, xi_ref, fr_ref, fi_ref, tr_ref, ti_ref, or_ref, oi_ref = refs
    else:
        xr_ref, fr_ref, fi_ref, tr_ref, ti_ref, or_ref, oi_ref = refs
    fr, fi = fr_ref[...], fi_ref[...]
    xr = xr_ref[...].astype(BF16)
    ar = _fdot(fr, xr)
    ai = _fdot(fi, xr)
    if has_im:
        xi = xi_ref[...].astype(BF16)
        ar = ar - _fdot(fi, xi)
        ai = ai + _fdot(fr, xi)
    for j in range(tn2):
        sl = slice(j * c, (j + 1) * c)
        twr = _lane_tile(tr_ref[j], c // LANES)
        twi = _lane_tile(ti_ref[j], c // LANES)
        or_ref[:, sl] = (ar[:, sl] * twr - ai[:, sl] * twi).astype(or_ref.dtype)
        oi_ref[:, sl] = (ar[:, sl] * twi + ai[:, sl] * twr).astype(oi_ref.dtype)


def _fft_s1(x, n1, n2, c, tn2):
    has_im = x.ndim == 3
    n1in = x.shape[-2]
    fr, fi = _dft_mats(n1, -1.0)
    fr, fi = fr[:, :n1in], fi[:, :n1in]
    tr, ti = _twiddle(n1, n2, False)
    if has_im:
        xspecs = [pl.BlockSpec((None, n1in, tn2 * c), lambda j: (0, 0, j)),
                  pl.BlockSpec((None, n1in, tn2 * c), lambda j: (1, 0, j))]
    else:
        xspecs = [pl.BlockSpec((n1in, tn2 * c), lambda j: (0, j))]
    fspec = pl.BlockSpec((n1, n1in), lambda j: (0, 0))
    tspec = pl.BlockSpec((tn2, n1, LANES), lambda j: (j, 0, 0))
    ospec = pl.BlockSpec((n1, tn2 * c), lambda j: (0, j))
    return pl.pallas_call(
        functools.partial(_fft_s1_kernel, has_im=has_im, tn2=tn2, c=c),
        out_shape=(jax.ShapeDtypeStruct((n1, n2 * c), BF16),) * 2,
        grid=(n2 // tn2,),
        in_specs=xspecs + [fspec, fspec, tspec, tspec],
        out_specs=(ospec, ospec),
        compiler_params=_cparams(("parallel",)),
        name="fft_stage1",
    )(*([x] * len(xspecs)), fr, fi, tr, ti)


def _fft_is1_kernel(er_ref, ei_ref, gr_ref, gi_ref, v_ref, x_ref, b_ref, z_ref, *, tn2, c, scale):
    gr, gi = gr_ref[...], gi_ref[...]
    er, ei = er_ref[...], ei_ref[...]
    yr = (_fdot(gr, er) - _fdot(gi, ei)) * scale
    yi = (_fdot(gr, ei) + _fdot(gi, er)) * scale
    bias = _lane_tile(b_ref[...], tn2)
    z_ref[0] = ((yr + v_ref[0].astype(F32) * bias) * x_ref[0].astype(F32)).astype(z_ref.dtype)
    z_ref[1] = ((yi + v_ref[1].astype(F32) * bias) * x_ref[1].astype(F32)).astype(z_ref.dtype)


def _fft_is1_gate(er, ei, v2, x2, bias, n1, n2, c, tn2):
    n1out = n1 // 2
    gr, gi = _dft_mats(n1, +1.0)
    gr, gi = gr[:n1out], gi[:n1out]
    dspec = pl.BlockSpec((n1, tn2 * c), lambda j: (0, j))
    gspec = pl.BlockSpec((n1out, n1), lambda j: (0, 0))
    sspec = pl.BlockSpec((2, n1out, tn2 * c), lambda j: (0, 0, j))
    return pl.pallas_call(
        functools.partial(_fft_is1_kernel, tn2=tn2, c=c, scale=1.0 / (n1 * n2)),
        out_shape=jax.ShapeDtypeStruct((2, n1out, n2 * c), BF16),
        grid=(n2 // tn2,),
        in_specs=[dspec, dspec, gspec, gspec, sspec, sspec, pl.BlockSpec((1, c), lambda j: (0, 0))],
        out_specs=sspec,
        compiler_params=_cparams(("parallel",)),
        name="fft_inv_stage1_gate",
    )(er, ei, gr, gi, v2, x2, bias.reshape(1, c))


def _cdot(fr, fi, xr, xi):
    return _fdot(fr, xr) - _fdot(fi, xi), _fdot(fr, xi) + _fdot(fi, xr)


def _fft_s2_filter_kernel(ar_ref, ai_ref, fr_ref, fi_ref, nrm_ref, kr_ref, ki_ref, *, tk1):
    fr, fi = fr_ref[...], fi_ref[...]
    inv = 1.0 / nrm_ref[...]
    for q in range(tk1):
        br, bi = _cdot(fr, fi, ar_ref[q], ai_ref[q])
        kr_ref[q] = br * inv
        ki_ref[q] = bi * inv


def _fft_s2_filter(ar, ai, nrm, n2, tk1):
    n1, _, c = ar.shape
    fr, fi = _dft_mats(n2, -1.0)
    aspec = pl.BlockSpec((tk1, n2, c), lambda i: (i, 0, 0))
    fspec = pl.BlockSpec((n2, n2), lambda i: (0, 0))
    return pl.pallas_call(
        functools.partial(_fft_s2_filter_kernel, tk1=tk1),
        out_shape=(jax.ShapeDtypeStruct((n1, n2, c), F32),) * 2,
        grid=(n1 // tk1,),
        in_specs=[aspec, aspec, fspec, fspec, pl.BlockSpec((1, c), lambda i: (0, 0))],
        out_specs=(aspec, aspec),
        compiler_params=_cparams(("parallel",)),
        name="fft_stage2_filter",
    )(ar, ai, fr, fi, nrm)


def _fft_s2_conv_kernel(ar_ref, ai_ref, kr_ref, ki_ref, fr_ref, fi_ref, tr_ref, ti_ref, er_ref, ei_ref, *,
                        tk1, c):
    fr, fi = fr_ref[...], fi_ref[...]
    nfi = -fi
    for q in range(tk1):
        br, bi = _cdot(fr, fi, ar_ref[q], ai_ref[q])
        kr, ki = kr_ref[q], ki_ref[q]
        pr = br * kr - bi * ki
        pi = br * ki + bi * kr
        dr, di = _cdot(fr, nfi, pr, pi)
        twr = _lane_tile(tr_ref[q], c // LANES)
        twi = _lane_tile(ti_ref[q], c // LANES)
        er_ref[q] = (dr * twr + di * twi).astype(er_ref.dtype)
        ei_ref[q] = (di * twr - dr * twi).astype(ei_ref.dtype)


def _fft_s2_conv(ar, ai, kr, ki, order, n2, tk1):
    n1, _, c = ar.shape
    fr, fi = _dft_mats(n2, -1.0)
    tr, ti = _twiddle(n1, n2, True)
    aspec = pl.BlockSpec((tk1, n2, c), lambda i: (i, 0, 0))
    kspec = pl.BlockSpec((tk1, n2, c), lambda i: (i, 0, order))
    fspec = pl.BlockSpec((n2, n2), lambda i: (0, 0))
    tspec = pl.BlockSpec((tk1, n2, LANES), lambda i: (i, 0, 0))
    return pl.pallas_call(
        functools.partial(_fft_s2_conv_kernel, tk1=tk1, c=c),
        out_shape=(jax.ShapeDtypeStruct((n1, n2, c), BF16),) * 2,
        grid=(n1 // tk1,),
        in_specs=[aspec, aspec, kspec, kspec, fspec, fspec, tspec, tspec],
        out_specs=(aspec, aspec),
        compiler_params=_cparams(("parallel",)),
        name="fft_stage2_conv",
    )(ar, ai, kr, ki, fr, fi, tr, ti)


def _fft_plan(L, c):
    n = 2 * L
    n1 = 128 if n >= 2048 else 64
    n2 = n // n1
    tn2 = max(1, min(n2, 2048 // c))
    tk1 = 8 if n2 * c * 4 * 8 <= (2 << 20) else max(1, (2 << 20) // (n2 * c * 4))
    return n1, n2, tn2, tk1


def _hyena(u, filt, hy_bias):
    bsz, L, c3 = u.shape
    assert bsz == 2
    c = c3 // 3
    n1, n2, tn2, tk1 = _fft_plan(L, c)
    n1h = n1 // 2
    view = lambda a: a.reshape(bsz, n1h, n2 * c)
    v, x1, x2 = view(u[:, :, :c]), view(u[:, :, c:2 * c]), view(u[:, :, 2 * c:])

    kt, nrm = _hyena_filter(L, *filt)
    cf = 2 * c
    _, _, tn2f, tk1f = _fft_plan(L, cf)
    far, fai = _fft_s1(kt.reshape(n1, n2 * cf), n1, n2, cf, tn2f)
    kr, ki = _fft_s2_filter(far.reshape(n1, n2, cf), fai.reshape(n1, n2, cf), nrm, n2, tk1f)

    def conv_gate(s, g, order):
        ar, ai = _fft_s1(s, n1, n2, c, tn2)
        er, ei = _fft_s2_conv(ar.reshape(n1, n2, c), ai.reshape(n1, n2, c), kr, ki, order, n2, tk1)
        return _fft_is1_gate(er.reshape(n1, n2 * c), ei.reshape(n1, n2 * c), s, g, hy_bias[order],
                             n1, n2, c, tn2)

    return conv_gate(conv_gate(v, x1, 0), x2, 1).reshape(bsz, L, c)


def _attn_prep_kernel(q_ref, k_ref, v_ref, c_ref, s1_ref, s2_ref, qs_ref, kr_ref, vt_ref, *, nh, qscale):
    cs, s1, s2 = c_ref[...], s1_ref[...], s2_ref[...]
    lane = lax.broadcasted_iota(jnp.int32, (1, LANES), 1)
    lo = lane < DA_HEAD_DIM

    def rope(x):
        return x * cs + pltpu.roll(x, LANES - 16, 1) * s1 + pltpu.roll(x, 16, 1) * s2

    ones_rows = jnp.ones((8, q_ref.shape[0]), F32)
    for h in range(nh):
        sl = slice(h * LANES, (h + 1) * LANES)
        q = rope(q_ref[:, sl].astype(F32)) * qscale
        qs_ref[h, 0] = jnp.where(lo, q, 0.0).astype(BF16)
        qs_ref[h, 1] = jnp.where(lo, 0.0, q).astype(BF16)
        kr_ref[:, sl] = rope(k_ref[:, sl].astype(F32)).astype(BF16)
        vt_ref[h] = jnp.concatenate([v_ref[:, sl].astype(F32).T, ones_rows], axis=0).astype(BF16)


def _rope_tables(nctx, L):
    f32 = np.float32
    rows = L // GRID_W
    r = np.repeat(np.arange(rows), GRID_W).astype(f32)
    col = np.tile(np.arange(GRID_W), rows).astype(f32)
    axis_dim = DA_HEAD_DIM // 2
    inv = np.power(f32(ROPE_THETA), -np.arange(0, axis_dim, 2, dtype=f32) / f32(axis_dim)).astype(f32)
    ar = r[:, None] * inv
    ac = col[:, None] * inv
    cr, sr, cc, sc = np.cos(ar), np.sin(ar), np.cos(ac), np.sin(ac)
    z = np.zeros_like(sr)
    cos64 = np.concatenate([cr, cr, cc, cc], axis=1)
    s1_64 = np.concatenate([-sr, z, -sc, z], axis=1)
    s2_64 = np.concatenate([z, sr, z, sc], axis=1)

    def full(t64, fill):
        t = np.concatenate([t64, t64], axis=1)
        return jnp.asarray(np.concatenate([np.full((nctx, LANES), fill, f32), t], axis=0).astype(f32))

    return full(cos64, 1.0), full(s1_64, 0.0), full(s2_64, 0.0)


def _attn_prep(proj, nctx, L, *, qcol, nh, tr=256):
    b, t, _ = proj.shape
    w = nh * LANES
    cs, s1, s2 = _rope_tables(nctx, L)
    qscale = (DA_HEAD_DIM ** -0.5) * math.log2(math.e)
    tspec = pl.BlockSpec((tr, LANES), lambda bi, i: (i, 0))
    return pl.pallas_call(
        functools.partial(_attn_prep_kernel, nh=nh, qscale=qscale),
        out_shape=(jax.ShapeDtypeStruct((b, nh, 2, t, LANES), BF16),
                   jax.ShapeDtypeStruct((b, t, w), BF16),
                   jax.ShapeDtypeStruct((b, nh, LANES + 8, t), BF16)),
        grid=(b, t // tr),
        in_specs=[pl.BlockSpec((None, tr, w), lambda bi, i: (bi, i, qcol)),
                  pl.BlockSpec((None, tr, w), lambda bi, i: (bi, i, qcol + 1)),
                  pl.BlockSpec((None, tr, w), lambda bi, i: (bi, i, qcol + 2)),
                  tspec, tspec, tspec],
        out_specs=(pl.BlockSpec((None, nh, 2, tr, LANES), lambda bi, i: (bi, 0, 0, i, 0)),
                   pl.BlockSpec((None, tr, w), lambda bi, i: (bi, i, 0)),
                   pl.BlockSpec((None, nh, LANES + 8, tr), lambda bi, i: (bi, 0, 0, i))),
        compiler_params=_cparams(("parallel", "parallel")),
        name="attn_prep",
    )(proj, proj, proj, cs, s1, s2)


def _attn_kernel(lam_ref, sw_ref, q_ref, k_ref, vt_ref, o_ref, s0_ref, s1_ref, acc_ref, *,
                 tq, nctx, tk, npairs, lam_init):
    i = pl.program_id(2)
    q = q_ref[...].reshape(2 * tq, LANES)

    def scores(j, s_ref):
        start = pl.multiple_of(j * tk, tk)
        s = _nt_dot(k_ref[pl.ds(start, tk), :], q)
        s_ref[...] = s
        return jnp.max(s, axis=0, keepdims=True)

    def consume(j, s_ref, m, alpha):
        start = pl.multiple_of(j * tk, tk)
        p = jnp.exp2(s_ref[...] - m).astype(BF16)
        acc_ref[...] = acc_ref[...] * alpha + jnp.dot(vt_ref[:, pl.ds(start, tk)], p,
                                                      preferred_element_type=F32)

    @pl.when(i * tq < nctx)
    def _():
        s = _nt_dot(k_ref[0:nctx, :], q)
        p = jnp.exp2(s - jnp.max(s, axis=0, keepdims=True)).astype(BF16)
        acc_ref[...] = jnp.dot(vt_ref[:, 0:nctx], p, preferred_element_type=F32)

    @pl.when(i * tq >= nctx)
    def _():
        acc_ref[...] = jnp.zeros_like(acc_ref)
        m0 = scores(0, s0_ref)

        def pair(pi, carry):
            m, alpha = carry
            m1 = jnp.maximum(m, scores(2 * pi + 1, s1_ref))
            consume(2 * pi, s0_ref, m, alpha)
            m2 = jnp.maximum(m1, scores(2 * pi + 2, s0_ref))
            consume(2 * pi + 1, s1_ref, m1, jnp.exp2(m - m1))
            return m2, jnp.exp2(m1 - m2)

        m, alpha = lax.fori_loop(0, npairs, pair, (m0, jnp.ones_like(m0)), unroll=True)
        consume(2 * npairs, s0_ref, m, alpha)

    acc = acc_ref[...]
    o = acc[0:LANES, :] / acc[LANES:LANES + 1, :]
    lp = lam_ref[...]
    lam = (jnp.exp(jnp.sum(lp[0:1] * lp[1:2], axis=1, keepdims=True))
           - jnp.exp(jnp.sum(lp[2:3] * lp[3:4], axis=1, keepdims=True)) + lam_init)
    a = (o[:, 0:tq] - lam * o[:, tq:2 * tq]).T
    y = a * lax.rsqrt(jnp.mean(a * a, axis=-1, keepdims=True) + EPS) * sw_ref[...]
    o_ref[...] = y * (1.0 - lam_init)


def _diff_attention(qs, kr, vt, lam_p, subln_w, *, nctx, lam_init, tq=256):
    b, nh, _, t, _ = qs.shape
    tk = next(c for c in (1280, 1024, 768, 640, 512, 384, 256, 128) if t % c == 0 and (t // c) % 2 == 1)
    assert nctx % tq == 0 and nctx % LANES == 0
    return pl.pallas_call(
        functools.partial(_attn_kernel, tq=tq, nctx=nctx, tk=tk, npairs=(t // tk) // 2, lam_init=lam_init),
        out_shape=jax.ShapeDtypeStruct((b, t, nh * LANES), F32),
        grid=(b, nh, t // tq),
        in_specs=[pl.BlockSpec((4, DA_HEAD_DIM), lambda bi, h, i: (0, 0)),
                  pl.BlockSpec((1, LANES), lambda bi, h, i: (0, 0)),
                  pl.BlockSpec((None, None, 2, tq, LANES), lambda bi, h, i: (bi, h, 0, i, 0)),
                  pl.BlockSpec((None, t, LANES), lambda bi, h, i: (bi, 0, h)),
                  pl.BlockSpec((None, None, LANES + 8, t), lambda bi, h, i: (bi, h, 0, 0))],
        out_specs=pl.BlockSpec((None, tq, LANES), lambda bi, h, i: (bi, i, h)),
        scratch_shapes=[pltpu.VMEM((tk, 2 * tq), F32), pltpu.VMEM((tk, 2 * tq), F32),
                        pltpu.VMEM((LANES + 8, 2 * tq), F32)],
        compiler_params=_cparams(("parallel", "parallel", "arbitrary")),
        name="diff_attention",
    )(lam_p, subln_w.reshape(1, LANES), qs, kr, vt)


def _merge_kernel(hy_ref, hg_ref, at_ref, ag_ref, h_ref, ml_ref, mc_ref, w1_ref, w2_ref, o_ref, *,
                  tm, nctx, d):
    i = pl.program_id(1)
    m1 = (hy_ref[...].astype(F32) * _silu(hg_ref[...].astype(F32))).astype(BF16)
    m2 = (at_ref[...] * _silu(ag_ref[...].astype(F32))).astype(BF16)
    out = (jnp.dot(m1, w1_ref[...], preferred_element_type=F32)
           + jnp.dot(m2, w2_ref[...], preferred_element_type=F32))
    g = _row_select(i, tm, nctx, mc_ref[:, 2 * d:3 * d], ml_ref[:, 2 * d:3 * d])
    o_ref[...] = h_ref[...] + g * out


def _merge_even(hyo, att, proj, h, modl, modc, out_w, *, nctx, hgcol, agcol):
    b, t, d = h.shape
    tm = _row_tile(t)
    c = hyo.shape[2]
    w1, w2 = out_w[:c].astype(BF16), out_w[c:].astype(BF16)
    rspec = lambda col: pl.BlockSpec((None, tm, c), lambda bi, i: (bi, i, col))
    return pl.pallas_call(
        functools.partial(_merge_kernel, tm=tm, nctx=nctx, d=d),
        out_shape=jax.ShapeDtypeStruct((b, t, d), F32),
        grid=(b, t // tm),
        in_specs=[rspec(0), rspec(hgcol), rspec(0), rspec(agcol),
                  pl.BlockSpec((None, tm, d), lambda bi, i: (bi, i, 0)),
                  pl.BlockSpec((None, 1, 3 * d), lambda bi, i: (bi, 0, 0)),
                  pl.BlockSpec((1, 3 * d), lambda bi, i: (0, 0)),
                  pl.BlockSpec((c, d), lambda bi, i: (0, 0)),
                  pl.BlockSpec((att.shape[2], d), lambda bi, i: (0, 0))],
        out_specs=pl.BlockSpec((None, tm, d), lambda bi, i: (bi, i, 0)),
        compiler_params=_cparams(("parallel", "parallel")),
        name="merge_even",
    )(hyo, proj, att, proj, h, modl, modc, w1, w2)


def _softplus(x):
    return jnp.maximum(x, 0.0) + jnp.log(1.0 + jnp.exp(-jnp.abs(x)))


def _split3_dot(m01, a):
    a1 = a.astype(BF16)
    r1 = a - a1.astype(F32)
    a2 = r1.astype(BF16)
    a3 = (r1 - a2.astype(F32)).astype(BF16)
    return (jnp.dot(m01, a1, preferred_element_type=F32) + jnp.dot(m01, a2, preferred_element_type=F32)
            + jnp.dot(m01, a3, preferred_element_type=F32))


def _ssd_kernel(*refs, reverse, nh, off):
    if reverse:
        x_ref, b_ref, c_ref, dt_ref, yin_ref, dtb_ref, alog_ref, dx_ref, y_ref, st_ref = refs
    else:
        x_ref, b_ref, c_ref, dt_ref, dtb_ref, alog_ref, dx_ref, y_ref, st_ref = refs
        yin_ref = None
    q = SSD_CHUNK
    p = SSD_HEAD_DIM
    hg = nh // SSD_GROUPS
    assert LANES == 2 * p

    @pl.when(pl.program_id(1) == 0)
    def _():
        st_ref[...] = jnp.zeros_like(st_ref)

    ri = lax.broadcasted_iota(jnp.int32, (q, q), 0)
    ci = lax.broadcasted_iota(jnp.int32, (q, q), 1)
    keep = (ci >= ri) if reverse else (ci <= ri)
    tri = jnp.where(keep, 1.0, 0.0).astype(BF16)

    dtv = _softplus(dt_ref[...] + dtb_ref[...])
    a = dtv * (-jnp.exp(alog_ref[...]))
    acum = _split3_dot(tri, a)
    acum_t = acum.T
    dt_t = dtv.T
    tot = acum[0:1, :] if reverse else acum[q - 1:q, :]
    etot = jnp.exp(tot)

    lane = lax.broadcasted_iota(jnp.int32, (1, LANES), 1)
    lo = lane < p
    for g in range(SSD_GROUPS):
        gs = slice(g * SSD_STATE, (g + 1) * SSD_STATE)
        bg = b_ref[:, gs].astype(F32)
        cg = c_ref[:, gs].astype(F32)
        cb = _nt_dot(c_ref[:, gs].astype(BF16), b_ref[:, gs].astype(BF16))
        bg_t = bg.T
        for pr in range(hg * p // LANES):
            ps = slice(g * hg * p + pr * LANES, g * hg * p + (pr + 1) * LANES)
            xb = x_ref[:, ps].astype(BF16)
            sp = st_ref[:, ps]
            rhs = jnp.concatenate([xb, sp.astype(BF16)], axis=0)
            lhs, wl, et = [], [], []
            for k in range(2):
                col = off + g * hg + pr * 2 + k
                ac_b = jnp.broadcast_to(acum[:, col:col + 1], (q, q))
                ac_row = acum_t[col:col + 1, :]
                dt_row = dt_t[col:col + 1, :]
                mm = cb * jnp.where(keep, jnp.exp(ac_b - ac_row), 0.0) * dt_row
                cs = cg * jnp.exp(ac_b)
                lhs.append(jnp.concatenate([mm, cs], axis=1).astype(BF16))
                w_row = dt_row * jnp.exp(tot[:, col:col + 1] - ac_row)
                wl.append((bg_t * w_row).astype(BF16))
                et.append(etot[:, col:col + 1])
            yy = jnp.dot(jnp.concatenate(lhs, axis=0), rhs, preferred_element_type=F32)
            ss = jnp.dot(jnp.concatenate(wl, axis=0), xb, preferred_element_type=F32)
            y_acc = jnp.where(lo, yy[0:q], yy[q:2 * q]) + x_ref[:, ps].astype(F32) * dx_ref[:, ps]
            if reverse:
                y_acc = y_acc + yin_ref[:, ps]
            y_ref[:, ps] = y_acc
            st_ref[:, ps] = (sp * jnp.where(lo, et[0], et[1])
                             + jnp.where(lo, ss[0:SSD_STATE], ss[SSD_STATE:2 * SSD_STATE]))


def _ssd_scan(xbc, dt, dt_bias, a_log, d_skip, y_prev, *, reverse, nctx):
    b, t, _ = xbc.shape
    nh = dt_bias.shape[1]
    inner = nh * SSD_HEAD_DIM
    gn = SSD_GROUPS * SSD_STATE
    q = SSD_CHUNK
    nct, ncc = t // q, nctx // q
    off = nh if reverse else 0
    if reverse:
        cidx = lambda s: jnp.where(s < ncc, ncc - 1 - s, nct - 1 - (s - ncc))
    else:
        cidx = lambda s: s
    pad = lambda v: jnp.pad(v.reshape(1, -1), ((0, 0), (0, LANES - 2 * nh)))
    d_x = jnp.repeat(d_skip[1 if reverse else 0], SSD_HEAD_DIM)[None, :]
    assert inner % gn == 0
    xspec = pl.BlockSpec((None, q, inner), lambda bi, s: (bi, cidx(s), 0))
    bspec = pl.BlockSpec((None, q, gn), lambda bi, s: (bi, cidx(s), inner // gn))
    cspec = pl.BlockSpec((None, q, gn), lambda bi, s: (bi, cidx(s), inner // gn + 1))
    dspec = pl.BlockSpec((None, q, LANES), lambda bi, s: (bi, cidx(s), 0))
    vspec = pl.BlockSpec((1, LANES), lambda bi, s: (0, 0))
    ins = [xbc, xbc, xbc, dt] + ([y_prev] if reverse else [])
    specs = [xspec, bspec, cspec, dspec] + ([xspec] if reverse else [])
    ins += [pad(dt_bias), pad(a_log), d_x]
    specs += [vspec, vspec, pl.BlockSpec((1, inner), lambda bi, s: (0, 0))]
    return pl.pallas_call(
        functools.partial(_ssd_kernel, reverse=reverse, nh=nh, off=off),
        out_shape=jax.ShapeDtypeStruct((b, t, inner), F32),
        grid=(b, nct),
        in_specs=specs,
        out_specs=xspec,
        scratch_shapes=[pltpu.VMEM((SSD_STATE, inner), F32)],
        compiler_params=_cparams(("parallel", "arbitrary")),
        name="ssd_bwd" if reverse else "ssd_fwd",
    )(*ins)


def _finish_kernel(y_ref, z_ref, gw_ref, w_ref, h_ref, ml_ref, fw_ref, o_ref, yn_ref, *, d, gsz):
    y = y_ref[...] * _silu(z_ref[...].astype(F32))
    for g in range(y.shape[1] // gsz):
        sl = slice(g * gsz, (g + 1) * gsz)
        yg = y[:, sl]
        yn_ref[:, sl] = (yg * lax.rsqrt(jnp.mean(yg * yg, axis=-1, keepdims=True) + EPS)
                         * gw_ref[:, sl]).astype(BF16)
    out = jnp.dot(yn_ref[...], w_ref[...], preferred_element_type=F32)
    hn = h_ref[...] + ml_ref[:, 2 * d:3 * d] * out
    o_ref[...] = hn * lax.rsqrt(jnp.mean(hn * hn, axis=-1, keepdims=True) + EPS) * fw_ref[...]


def _finish_odd(y, proj, h, modl, gnorm_w, out_w, final_w, *, nctx, tm=512):
    b, t, d = h.shape
    inner = y.shape[2]
    L = t - nctx
    tm = min(tm, nctx)
    assert nctx % tm == 0 and L % tm == 0
    r0 = nctx // tm
    return pl.pallas_call(
        functools.partial(_finish_kernel, d=d, gsz=inner // SSD_GROUPS),
        out_shape=jax.ShapeDtypeStruct((b, L, d), F32),
        grid=(b, L // tm),
        in_specs=[pl.BlockSpec((None, tm, inner), lambda bi, i: (bi, i + r0, 0)),
                  pl.BlockSpec((None, tm, inner), lambda bi, i: (bi, i + r0, 0)),
                  pl.BlockSpec((1, inner), lambda bi, i: (0, 0)),
                  pl.BlockSpec((inner, d), lambda bi, i: (0, 0)),
                  pl.BlockSpec((None, tm, d), lambda bi, i: (bi, i + r0, 0)),
                  pl.BlockSpec((None, 1, 3 * d), lambda bi, i: (bi, 0, 0)),
                  pl.BlockSpec((1, d), lambda bi, i: (0, 0))],
        out_specs=pl.BlockSpec((None, tm, d), lambda bi, i: (bi, i, 0)),
        scratch_shapes=[pltpu.VMEM((tm, inner), BF16)],
        compiler_params=_cparams(("parallel", "parallel")),
        name="finish_odd",
    )(y, proj, gnorm_w.reshape(1, inner), out_w.astype(BF16), h, modl, final_w.reshape(1, d))


def kernel(x, c, ctx, c_ctx, ada_w, ada_b, norm_w, ev_in_w, ev_out_w, hy_short_w, hy_short_b, hy_f_w1, hy_f_b1, hy_f_w2, hy_f_b2, hy_f_w3, hy_f_b3, hy_freq, hy_bias, da_lambda, da_subln_w, od_in_w, od_conv_w, od_conv_b, od_dt_bias, od_A_log, od_D, od_norm_w, od_out_w, final_norm_w):
    bsz, L, d = x.shape
    nctx = ctx.shape[1]
    depth = ada_w.shape[0]
    assert depth == 2 and bsz == 2
    hyw = hy_bias.shape[2]
    daw = ev_in_w.shape[2] // 4 - hyw
    nh_att = daw // LANES

    h = jnp.concatenate([ctx, x], axis=1)
    cstack = jnp.concatenate([c, c_ctx[None, :], jnp.zeros((8 - bsz - 1, d), F32)], axis=0)
    mod = _modulation(cstack, ada_w, ada_b)

    modl, modc = mod[0, :bsz, None, :], mod[0, bsz:bsz + 1, :]
    proj = _normproj(h, modl, modc, norm_w[0], ev_in_w[0].astype(BF16), nctx=nctx, out_dtype=BF16)
    assert hyw == daw
    u = _dwconv3(proj, hy_short_w[0], hy_short_b[0], col0=0, ncol=3, cw=hyw, nctx=nctx, act=False)
    filt = (hy_f_w1[0], hy_f_b1[0], hy_f_w2[0], hy_f_b2[0], hy_f_w3[0], hy_f_b3[0], hy_freq[0])
    hy_l = _hyena(u[:, nctx:], filt, hy_bias[0])
    hy_c = _hyena(u[:, :nctx], filt, hy_bias[0])
    hyo = jnp.concatenate([hy_c, hy_l], axis=1)

    qs, kr, vt = _attn_prep(proj, nctx, L, qcol=4, nh=nh_att)
    lam_init = 0.8 - 0.6 * math.exp(-0.3 * 0)
    att = _diff_attention(qs, kr, vt, da_lambda[0], da_subln_w[0], nctx=nctx, lam_init=lam_init)
    h = _merge_even(hyo, att, proj, h, modl, modc, ev_out_w[0], nctx=nctx, hgcol=3, agcol=7)

    modl, modc = mod[1, :bsz, None, :], mod[1, bsz:bsz + 1, :]
    nh = od_dt_bias.shape[2]
    inner = nh * SSD_HEAD_DIM
    convch = inner + 2 * SSD_GROUPS * SSD_STATE
    w_in = od_in_w[0]
    w_main = w_in[:, :inner + convch].astype(BF16)
    w_dt = jnp.pad(w_in[:, inner + convch:], ((0, 0), (0, LANES - 2 * nh))).astype(BF16)
    proj2 = _normproj(h, modl, modc, norm_w[1], w_main, nctx=nctx, out_dtype=BF16)
    dt = _normproj(h, modl, modc, norm_w[1], w_dt, nctx=nctx)
    xbc = _dwconv3(proj2, od_conv_w[0], od_conv_b[0], col0=1, ncol=convch // inner, cw=inner,
                   nctx=nctx, act=True)
    y = _ssd_scan(xbc, dt, od_dt_bias[0], od_A_log[0], od_D[0], None, reverse=False, nctx=nctx)
    y = _ssd_scan(xbc, dt, od_dt_bias[0], od_A_log[0], od_D[0], y, reverse=True, nctx=nctx)
    return _finish_odd(y, proj2, h, modl, od_norm_w[0], od_out_w[0], final_norm_w, nctx=nctx)
```
